```python
import jax, jax.numpy as jnp
from jax import lax
import numpy as np

D_MODEL = 1024
BATCH = 16
SEQ = 2048
DEPTH = 4
DEC_BATCH = 128
DEC_SEQ = 8
PAST_LEN = 8192
PAGE_SIZE = 128

SSD_EXPAND = 2
D_INNER = SSD_EXPAND * D_MODEL
SSD_HEAD_DIM = 64
SSD_HEADS = D_INNER // SSD_HEAD_DIM
SSD_GROUPS = 4
SSD_HEADS_PER_GROUP = SSD_HEADS // SSD_GROUPS
SSD_STATE = 128
SSD_CONV = 4
SSD_CHUNK = 128
CONV_DIM = D_INNER + 2 * SSD_GROUPS * SSD_STATE
D_IN_PROJ = D_INNER + CONV_DIM + SSD_HEADS
MLA_HEADS = 8
Q_LORA = 384
KV_LORA = 256
QK_NOPE = 128
QK_ROPE = 64
V_HEAD = 128
ROPE_THETA = 10000.0
ATTN_Q_BLOCK = 128
MOE_GROUPS = 4
EXPERTS_PER_GROUP = 8
N_EXPERTS = MOE_GROUPS * EXPERTS_PER_GROUP
EXPERT_FF = 256
GROUP_TOPK_SCORE = 2
EXPERT_TOPK = 2
N_SSD_LAYERS = (DEPTH + 1) // 2
N_MLA_LAYERS = DEPTH // 2
DEEPNORM_ALPHA = (2.0 * DEPTH) ** 0.25
DEEPNORM_BETA = (8.0 * DEPTH) ** -0.25
LN_EPS = 1e-5
RMS_EPS = 1e-6

kernel_name = 'hybrid_ssd_mla_hmoe_step'


def layer_norm(x, g, b):
    xf = x.astype(jnp.float32)
    mu = jnp.mean(xf, axis=-1, keepdims=True)
    var = jnp.mean(jnp.square(xf - mu), axis=-1, keepdims=True)
    return ((xf - mu) * lax.rsqrt(var + LN_EPS) * g.astype(jnp.float32) + b.astype(jnp.float32)).astype(x.dtype)


def rms_norm(x, g):
    xf = x.astype(jnp.float32)
    out = xf * lax.rsqrt(jnp.mean(jnp.square(xf), axis=-1, keepdims=True) + RMS_EPS) * g.astype(jnp.float32)
    return out.astype(x.dtype)


def rope_tables(pos):
    half = QK_ROPE // 2
    inv_freq = ROPE_THETA ** (-jnp.arange(half, dtype=jnp.float32) / half)
    ang = pos.astype(jnp.float32)[:, None] * inv_freq[None, :]
    return jnp.cos(ang), jnp.sin(ang)


def apply_rope(x, cos, sin):
    x1, x2 = jnp.split(x.astype(jnp.float32), 2, axis=-1)
    return jnp.concatenate([x1 * cos - x2 * sin, x2 * cos + x1 * sin], axis=-1).astype(x.dtype)


def causal_conv(xbc, conv_state, w, b):
    t = xbc.shape[1]
    xp = jnp.concatenate([conv_state.astype(xbc.dtype), xbc], axis=1)
    out = b
    for k in range(SSD_CONV):
        out = out + xp[:, k:k + t] * w[k]
    return jax.nn.silu(out), xp[:, t:]


def ssd_scan(xs, dt, a_neg, bm, cm, h0):
    bsz, t = xs.shape[:2]
    lc = min(SSD_CHUNK, t)
    nc = -(-t // lc)
    pad = nc * lc - t

    def prep(a):
        a = a.astype(jnp.float32)
        a = jnp.pad(a, [(0, 0), (0, pad)] + [(0, 0)] * (a.ndim - 2))
        return jnp.moveaxis(a.reshape((bsz, nc, lc) + a.shape[2:]), 1, 0)

    causal = jnp.tril(jnp.ones((lc, lc), dtype=bool))[None, :, :, None, None]

    def chunk_step(h, inp):
        xc, dtc, bc, cc = inp
        cum = jnp.cumsum(dtc * a_neg, axis=1)
        seg = cum[:, :, None] - cum[:, None, :]
        decay = jnp.exp(jnp.where(causal, seg, -jnp.inf))
        cb = jnp.einsum('blgn,bsgn->blsg', cc, bc)
        xdt = xc * dtc[..., None]
        y = jnp.einsum('blsgj,bsgjp->blgjp', cb[..., None] * decay, xdt)
        y = y + jnp.einsum('blgn,bgjpn->blgjp', cc, h) * jnp.exp(cum)[..., None]
        last = cum[:, -1]
        w_s = jnp.exp(last[:, None] - cum)[..., None]
        h = h * jnp.exp(last)[..., None, None] + jnp.einsum('bsgn,bsgjp->bgjpn', bc, xdt * w_s)
        return h, y

    h_fin, y = lax.scan(chunk_step, h0.astype(jnp.float32), (prep(xs), prep(dt), prep(bm), prep(cm)))
    y = jnp.moveaxis(y, 0, 1).reshape((bsz, nc * lc) + xs.shape[2:])[:, :t]
    return y.astype(xs.dtype), h_fin.astype(h0.dtype)


def ssd_mixer(x, conv_state, ssm_state, w_in, conv_w, conv_b, dt_bias, a_log, d_skip, norm_w, w_out):
    bsz, t, _ = x.shape
    z, xbc, dt = jnp.split(x @ w_in, [D_INNER, D_INNER + CONV_DIM], axis=-1)
    xbc, new_conv = causal_conv(xbc, conv_state, conv_w, conv_b)
    xs, bm, cm = jnp.split(xbc, [D_INNER, D_INNER + SSD_GROUPS * SSD_STATE], axis=-1)
    xs = xs.reshape(bsz, t, SSD_GROUPS, SSD_HEADS_PER_GROUP, SSD_HEAD_DIM)
    bm = bm.reshape(bsz, t, SSD_GROUPS, SSD_STATE)
    cm = cm.reshape(bsz, t, SSD_GROUPS, SSD_STATE)
    dt = jax.nn.softplus(dt.astype(jnp.float32) + dt_bias.astype(jnp.float32))
    dt = dt.reshape(bsz, t, SSD_GROUPS, SSD_HEADS_PER_GROUP)
    a_neg = -jnp.exp(a_log.astype(jnp.float32)).reshape(SSD_GROUPS, SSD_HEADS_PER_GROUP)
    h0 = ssm_state.reshape(bsz, SSD_GROUPS, SSD_HEADS_PER_GROUP, SSD_HEAD_DIM, SSD_STATE)
    y, h = ssd_scan(xs, dt, a_neg, bm, cm, h0)
    y = y + d_skip.reshape(SSD_GROUPS, SSD_HEADS_PER_GROUP)[..., None] * xs
    gated = (y.reshape(bsz, t, D_INNER) * jax.nn.silu(z)).astype(jnp.float32)
    gated = gated.reshape(bsz, t, SSD_GROUPS, D_INNER // SSD_GROUPS)
    gated = gated * lax.rsqrt(jnp.mean(jnp.square(gated), axis=-1, keepdims=True) + RMS_EPS)
    y = (gated.reshape(bsz, t, D_INNER) * norm_w.astype(jnp.float32)).astype(x.dtype)
    return y @ w_out, new_conv, h.reshape(bsz, SSD_HEADS, SSD_HEAD_DIM, SSD_STATE)


def latent_attention(q_lat, q_rope, k_lat, k_rope):
    bsz, t = q_lat.shape[:2]
    s = k_lat.shape[1]
    qb = min(ATTN_Q_BLOCK, t)
    nb = -(-t // qb)
    pad = nb * qb - t
    q_lat = jnp.pad(q_lat, [(0, 0), (0, pad), (0, 0), (0, 0)])
    q_rope = jnp.pad(q_rope, [(0, 0), (0, pad), (0, 0), (0, 0)])
    q_pos = ((s - t) + jnp.arange(nb * qb)).reshape(nb, qb)
    k_pos = jnp.arange(s)
    scale = (QK_NOPE + QK_ROPE) ** -0.5
    qlb = jnp.swapaxes(q_lat.reshape(bsz, nb, qb, MLA_HEADS, KV_LORA), 0, 1)
    qrb = jnp.swapaxes(q_rope.reshape(bsz, nb, qb, MLA_HEADS, QK_ROPE), 0, 1)

    def block(args):
        ql, qr, qp = args
        sc = jnp.einsum('bqhc,bkc->bhqk', ql, k_lat, preferred_element_type=jnp.float32)
        sc = sc + jnp.einsum('bqhr,bkr->bhqk', qr, k_rope, preferred_element_type=jnp.float32)
        sc = jnp.where(k_pos[None, :] <= qp[:, None], sc * scale, -jnp.inf)
        p = jax.nn.softmax(sc, axis=-1)
        return jnp.einsum('bhqk,bkc->bqhc', p.astype(k_lat.dtype), k_lat)

    o = lax.map(block, (qlb, qrb, q_pos))
    return jnp.swapaxes(o, 0, 1).reshape(bsz, nb * qb, MLA_HEADS, KV_LORA)[:, :t]


def mla_mixer(x, pos, past_lat, past_krope, w_dqkv, q_norm, kv_norm, w_uq, w_uk, w_uv, w_o):
    bsz, t, _ = x.shape
    cq, ckv, kr = jnp.split(x @ w_dqkv, [Q_LORA, Q_LORA + KV_LORA], axis=-1)
    cq = rms_norm(cq, q_norm)
    ckv = rms_norm(ckv, kv_norm)
    q = jnp.einsum('btc,chd->bthd', cq, w_uq)
    q_nope, q_rope = jnp.split(q, [QK_NOPE], axis=-1)
    cos, sin = rope_tables(pos)
    q_rope = apply_rope(q_rope, cos[:, None, :], sin[:, None, :])
    kr = apply_rope(kr, cos, sin)
    q_lat = jnp.einsum('bthn,chn->bthc', q_nope, w_uk)
    if past_lat is None:
        k_lat, k_rope = ckv, kr
    else:
        k_lat = jnp.concatenate([past_lat.astype(ckv.dtype), ckv], axis=1)
        k_rope = jnp.concatenate([past_krope.astype(kr.dtype), kr], axis=1)
    o_lat = latent_attention(q_lat, q_rope, k_lat, k_rope)
    o = jnp.einsum('bthc,chv->bthv', o_lat, w_uv).reshape(bsz, t, MLA_HEADS * V_HEAD)
    return o @ w_o, ckv, kr


def hier_moe(x, router, router_bias, w_gate, w_up, w_down):
    bsz, t, d = x.shape
    xt = x.reshape(-1, d)
    n = xt.shape[0]
    aff = jax.nn.sigmoid((xt @ router).astype(jnp.float32))
    sel = (aff + router_bias.astype(jnp.float32)).reshape(n, MOE_GROUPS, EXPERTS_PER_GROUP)
    group_score = jnp.sum(lax.top_k(sel, GROUP_TOPK_SCORE)[0], axis=-1)
    grp = jnp.argmax(group_score, axis=-1)
    gidx = jnp.broadcast_to(grp[:, None, None], (n, 1, EXPERTS_PER_GROUP))
    sel_in = jnp.take_along_axis(sel, gidx, axis=1)[:, 0]
    _, loc = lax.top_k(sel_in, EXPERT_TOPK)
    eidx = grp[:, None] * EXPERTS_PER_GROUP + loc
    wts = jnp.take_along_axis(aff, eidx, axis=1)
    wts = wts / jnp.sum(wts, axis=-1, keepdims=True)
    gate = jnp.einsum('nk,nke->ne', wts, jax.nn.one_hot(eidx, N_EXPERTS, dtype=jnp.float32))
    out = jnp.zeros((n, d), jnp.float32)
    for g in range(MOE_GROUPS):
        sl = slice(g * EXPERTS_PER_GROUP, (g + 1) * EXPERTS_PER_GROUP)
        hg = jax.nn.silu(jnp.einsum('nd,edf->nef', xt, w_gate[sl])) * jnp.einsum('nd,edf->nef', xt, w_up[sl])
        hg = hg * gate[:, sl, None].astype(hg.dtype)
        out = out + jnp.einsum('nef,efd->nd', hg, w_down[sl], preferred_element_type=jnp.float32)
    return out.astype(x.dtype).reshape(bsz, t, d)


def setup_inputs(seed: int = 0) -> dict:
    key = jax.random.key(seed)
    ks = jax.random.split(key, 40)
    f32 = jnp.float32

    def nrm(i, shape, scale):
        return jax.random.normal(ks[i], shape, f32) * scale

    n_pages = PAST_LEN // PAGE_SIZE
    n_used = DEC_BATCH * n_pages
    n_pool = n_used + n_used // 4
    perm = jax.random.permutation(ks[0], n_pool)
    page_table = perm[:n_used].reshape(DEC_BATCH, n_pages).astype(jnp.int32)

    dt0 = jnp.exp(jax.random.uniform(ks[1], (N_SSD_LAYERS, SSD_HEADS), f32, minval=np.log(1e-3), maxval=np.log(1e-1)))
    ssd_dt_bias = dt0 + jnp.log(-jnp.expm1(-dt0))
    ssd_a_log = jnp.log(jax.random.uniform(ks[2], (N_SSD_LAYERS, SSD_HEADS), f32, minval=1.0, maxval=16.0))

    return {
        'x_prompt': nrm(3, (BATCH, SEQ, D_MODEL), 1.0),
        'x_sample': nrm(4, (DEC_BATCH, DEC_SEQ, D_MODEL), 1.0),
        'cache_mla_latent': nrm(5, (N_MLA_LAYERS, n_pool, PAGE_SIZE, KV_LORA), 1.0),
        'cache_mla_krope': nrm(6, (N_MLA_LAYERS, n_pool, PAGE_SIZE, QK_ROPE), 1.0),
        'state_ssd': nrm(7, (N_SSD_LAYERS, DEC_BATCH, SSD_HEADS, SSD_HEAD_DIM, SSD_STATE), 0.1),
        'state_conv': nrm(8, (N_SSD_LAYERS, DEC_BATCH, SSD_CONV - 1, CONV_DIM), 1.0),
        'page_table': page_table,
        'ln_mix_g': 1.0 + nrm(9, (DEPTH, D_MODEL), 0.02),
        'ln_mix_b': nrm(10, (DEPTH, D_MODEL), 0.02),
        'ln_ffn_g': 1.0 + nrm(11, (DEPTH, D_MODEL), 0.02),
        'ln_ffn_b': nrm(12, (DEPTH, D_MODEL), 0.02),
        'ssd_w_in': nrm(13, (N_SSD_LAYERS, D_MODEL, D_IN_PROJ), D_MODEL ** -0.5),
        'ssd_conv_w': nrm(14, (N_SSD_LAYERS, SSD_CONV, CONV_DIM), SSD_CONV ** -0.5),
        'ssd_conv_b': nrm(15, (N_SSD_LAYERS, CONV_DIM), 0.02),
        'ssd_dt_bias': ssd_dt_bias,
        'ssd_a_log': ssd_a_log,
        'ssd_d_skip': 1.0 + nrm(16, (N_SSD_LAYERS, SSD_HEADS), 0.02),
        'ssd_norm_w': 1.0 + nrm(17, (N_SSD_LAYERS, D_INNER), 0.02),
        'ssd_w_out': nrm(18, (N_SSD_LAYERS, D_INNER, D_MODEL), D_INNER ** -0.5 * DEEPNORM_BETA),
        'mla_w_dqkv': nrm(19, (N_MLA_LAYERS, D_MODEL, Q_LORA + KV_LORA + QK_ROPE), D_MODEL ** -0.5),
        'mla_q_norm': 1.0 + nrm(20, (N_MLA_LAYERS, Q_LORA), 0.02),
        'mla_kv_norm': 1.0 + nrm(21, (N_MLA_LAYERS, KV_LORA), 0.02),
        'mla_w_uq': nrm(22, (N_MLA_LAYERS, Q_LORA, MLA_HEADS, QK_NOPE + QK_ROPE), Q_LORA ** -0.5),
        'mla_w_uk': nrm(23, (N_MLA_LAYERS, KV_LORA, MLA_HEADS, QK_NOPE), KV_LORA ** -0.5),
        'mla_w_uv': nrm(24, (N_MLA_LAYERS, KV_LORA, MLA_HEADS, V_HEAD), KV_LORA ** -0.5),
        'mla_w_o': nrm(25, (N_MLA_LAYERS, MLA_HEADS * V_HEAD, D_MODEL), (MLA_HEADS * V_HEAD) ** -0.5 * DEEPNORM_BETA),
        'moe_router': nrm(26, (DEPTH, D_MODEL, N_EXPERTS), D_MODEL ** -0.5),
        'moe_router_bias': nrm(27, (DEPTH, N_EXPERTS), 0.01),
        'moe_w_gate': nrm(28, (DEPTH, N_EXPERTS, D_MODEL, EXPERT_FF), D_MODEL ** -0.5),
        'moe_w_up': nrm(29, (DEPTH, N_EXPERTS, D_MODEL, EXPERT_FF), D_MODEL ** -0.5),
        'moe_w_down': nrm(30, (DEPTH, N_EXPERTS, EXPERT_FF, D_MODEL), EXPERT_FF ** -0.5 * DEEPNORM_BETA),
    }


def reference(x_prompt, x_sample, cache_mla_latent, cache_mla_krope, state_ssd, state_conv, page_table,
              ln_mix_g, ln_mix_b, ln_ffn_g, ln_ffn_b,
              ssd_w_in, ssd_conv_w, ssd_conv_b, ssd_dt_bias, ssd_a_log, ssd_d_skip, ssd_norm_w, ssd_w_out,
              mla_w_dqkv, mla_q_norm, mla_kv_norm, mla_w_uq, mla_w_uk, mla_w_uv, mla_w_o,
              moe_router, moe_router_bias, moe_w_gate, moe_w_up, moe_w_down):
    bsz_p, t_p, _ = x_prompt.shape
    bsz_s, t_s, _ = x_sample.shape
    pos_p = jnp.arange(t_p)
    pos_s = PAST_LEN + jnp.arange(t_s)
    conv0 = jnp.zeros((bsz_p, SSD_CONV - 1, CONV_DIM), x_prompt.dtype)
    ssm0 = jnp.zeros((bsz_p, SSD_HEADS, SSD_HEAD_DIM, SSD_STATE), x_prompt.dtype)

    p_lat, p_kr, p_ssm, p_conv = [], [], [], []
    s_lat, s_kr, s_ssm, s_conv = [], [], [], []
    xp, xs = x_prompt, x_sample
    for i in range(DEPTH):
        j = i // 2
        if i % 2 == 0:
            hp, cp, sp = ssd_mixer(xp, conv0, ssm0, ssd_w_in[j], ssd_conv_w[j], ssd_conv_b[j], ssd_dt_bias[j],
                                   ssd_a_log[j], ssd_d_skip[j], ssd_norm_w[j], ssd_w_out[j])
            hs, cs, ss = ssd_mixer(xs, state_conv[j], state_ssd[j], ssd_w_in[j], ssd_conv_w[j], ssd_conv_b[j],
                                   ssd_dt_bias[j], ssd_a_log[j], ssd_d_skip[j], ssd_norm_w[j], ssd_w_out[j])
            p_conv.append(cp)
            p_ssm.append(sp)
            s_conv.append(cs)
            s_ssm.append(ss)
        else:
            past_lat = jnp.take(cache_mla_latent[j], page_table, axis=0).reshape(bsz_s, -1, KV_LORA)
            past_kr = jnp.take(cache_mla_krope[j], page_table, axis=0).reshape(bsz_s, -1, QK_ROPE)
            hp, lp, kp = mla_mixer(xp, pos_p, None, None, mla_w_dqkv[j], mla_q_norm[j], mla_kv_norm[j],
                                   mla_w_uq[j], mla_w_uk[j], mla_w_uv[j], mla_w_o[j])
            hs, ls, kss = mla_mixer(xs, pos_s, past_lat, past_kr, mla_w_dqkv[j], mla_q_norm[j], mla_kv_norm[j],
                                    mla_w_uq[j], mla_w_uk[j], mla_w_uv[j], mla_w_o[j])
            p_lat.append(lp)
            p_kr.append(kp)
            s_lat.append(ls)
            s_kr.append(kss)
        xp = layer_norm(DEEPNORM_ALPHA * xp + hp, ln_mix_g[i], ln_mix_b[i])
        xs = layer_norm(DEEPNORM_ALPHA * xs + hs, ln_mix_g[i], ln_mix_b[i])
        fp = hier_moe(xp, moe_router[i], moe_router_bias[i], moe_w_gate[i], moe_w_up[i], moe_w_down[i])
        fs = hier_moe(xs, moe_router[i], moe_router_bias[i], moe_w_gate[i], moe_w_up[i], moe_w_down[i])
        xp = layer_norm(DEEPNORM_ALPHA * xp + fp, ln_ffn_g[i], ln_ffn_b[i])
        xs = layer_norm(DEEPNORM_ALPHA * xs + fs, ln_ffn_g[i], ln_ffn_b[i])

    return (xp, xs,
            jnp.stack(p_lat), jnp.stack(p_kr), jnp.stack(p_ssm), jnp.stack(p_conv),
            jnp.stack(s_lat), jnp.stack(s_kr), jnp.stack(s_ssm), jnp.stack(s_conv))
```

```python
import functools

import jax
import jax.numpy as jnp
from jax import lax
from jax.experimental import pallas as pl
from jax.experimental.pallas import tpu as pltpu

F32 = jnp.float32
BF16 = jnp.bfloat16
I32 = jnp.int32

D_MODEL = 1024
DEPTH = 4
PAGE_SIZE = 128
D_INNER = 2048
SSD_HEAD_DIM = 64
SSD_HEADS = 32
SSD_GROUPS = 4
SSD_STATE = 128
SSD_CONV = 4
SSD_CHUNK = 128
CONV_DIM = D_INNER + 2 * SSD_GROUPS * SSD_STATE
GROUP_INNER = D_INNER // SSD_GROUPS
MLA_HEADS = 8
Q_LORA = 384
KV_LORA = 256
QK_NOPE = 128
QK_ROPE = 64
V_HEAD = 128
ROPE_THETA = 10000.0
MOE_GROUPS = 4
EXPERTS_PER_GROUP = 8
N_EXPERTS = 32
EXPERT_FF = 256
DEEPNORM_ALPHA = (2.0 * DEPTH) ** 0.25
LN_EPS = 1e-5
RMS_EPS = 1e-6
ATTN_SCALE = (QK_NOPE + QK_ROPE) ** -0.5

LANES = 128
KCAT = KV_LORA + LANES
DCOLS = Q_LORA + KV_LORA + LANES
N_BUCKETS = 256
MAX_PAIRS = MOE_GROUPS * (EXPERTS_PER_GROUP * (EXPERTS_PER_GROUP - 1) // 2)
MOE_TILE = 128
VMEM_LIMIT = 56 * 1024 * 1024


def _cparams(*sem):
    return pltpu.CompilerParams(dimension_semantics=sem, vmem_limit_bytes=VMEM_LIMIT)


def _dot(a, b):
    return jnp.dot(a, b, preferred_element_type=F32)


def _dot_nt(a, b):
    return lax.dot_general(a, b, (((1,), (1,)), ((), ())), preferred_element_type=F32)


def _dot_tn(a, b):
    return lax.dot_general(a, b, (((0,), (0,)), ((), ())), preferred_element_type=F32)


def _split2(x):
    hi = x.astype(BF16)
    return hi, (x - hi.astype(F32)).astype(BF16)


def _split3(x):
    hi = x.astype(BF16)
    r = x - hi.astype(F32)
    mid = r.astype(BF16)
    return hi, mid, (r - mid.astype(F32)).astype(BF16)


def _silu(x):
    return x * jax.nn.sigmoid(x)


def _softplus(x):
    return jnp.maximum(x, 0.0) + jnp.log1p(jnp.exp(-jnp.abs(x)))


def _layer_norm(v, g, b):
    mu = jnp.mean(v, axis=-1, keepdims=True)
    d = v - mu
    var = jnp.mean(d * d, axis=-1, keepdims=True)
    return d * lax.rsqrt(var + LN_EPS) * g + b


def _rms_norm(v, g):
    return v * lax.rsqrt(jnp.mean(v * v, axis=-1, keepdims=True) + RMS_EPS) * g


def _rope128(x, cos, sin):
    lane = lax.broadcasted_iota(I32, x.shape, 1)
    partner = jnp.where(lane < QK_ROPE // 2, pltpu.roll(x, LANES - QK_ROPE // 2, 1), pltpu.roll(x, QK_ROPE // 2, 1))
    return x * cos + partner * sin


def _ssd_inproj_kernel(x_ref, wz_ref, wx_ref, wdh_ref, wdl_ref, z_ref, xbc_ref, dt_ref):
    x = x_ref[...]
    xh, xm = _split2(x)
    z_ref[...] = _dot(xh, wz_ref[...]).astype(z_ref.dtype)
    xbc_ref[...] = _dot(xh, wx_ref[...])
    wdh = wdh_ref[...]
    dt_ref[...] = _dot(xh, wdh) + _dot(xm, wdh) + _dot(xh, wdl_ref[...])


def _ssd_inproj(x, wz, wx, wdh, wdl, z_dtype, tm):
    m = x.shape[0]
    full = lambda a: pl.BlockSpec(a.shape, lambda i: (0, 0))
    row = lambda n: pl.BlockSpec((tm, n), lambda i: (i, 0))
    return pl.pallas_call(
        _ssd_inproj_kernel,
        grid=(m // tm,),
        in_specs=[row(D_MODEL), full(wz), full(wx), full(wdh), full(wdl)],
        out_specs=[row(D_INNER), row(CONV_DIM), row(LANES)],
        out_shape=[jax.ShapeDtypeStruct((m, D_INNER), z_dtype),
                   jax.ShapeDtypeStruct((m, CONV_DIM), F32),
                   jax.ShapeDtypeStruct((m, LANES), F32)],
        compiler_params=_cparams("parallel"),
        name="ssd_inproj",
    )(x, wz, wx, wdh, wdl)


def _ssd_kernel(*refs, L, has_init):
    if has_init:
        (xbc_ref, z_ref, dt_ref, cw_ref, cb_ref, dtb_ref, alog_ref, dskip_ref, nw_ref, e_ref,
         conv0_ref, ssm0_ref, y_ref, convo_ref, sso_ref, xp_scr, h_scr) = refs
    else:
        (xbc_ref, z_ref, dt_ref, cw_ref, cb_ref, dtb_ref, alog_ref, dskip_ref, nw_ref, e_ref,
         y_ref, convo_ref, sso_ref, xp_scr, h_scr) = refs
    c = pl.program_id(1)
    nc = pl.num_programs(1)
    LK = max(L, LANES)
    K1 = SSD_CONV - 1
    HPG = SSD_HEADS // SSD_GROUPS

    @pl.when(c == 0)
    def _():
        if has_init:
            xp_scr[8 - K1:8, :] = conv0_ref[...]
            h_scr[...] = ssm0_ref[...].reshape(D_INNER, SSD_STATE)
        else:
            xp_scr[0:8, :] = jnp.zeros((8, CONV_DIM), F32)
            h_scr[...] = jnp.zeros((D_INNER, SSD_STATE), F32)

    def pad_rows(a):
        if L == LK:
            return a
        return jnp.concatenate([a, jnp.zeros((LK - L, a.shape[1]), a.dtype)], axis=0)

    xraw = xbc_ref[...]
    xp_scr[8:8 + L, :] = xraw
    acc = cb_ref[...] + xp_scr[5:5 + L, :] * cw_ref[0:1, :]
    acc = acc + xp_scr[6:6 + L, :] * cw_ref[1:2, :]
    acc = acc + xp_scr[7:7 + L, :] * cw_ref[2:3, :]
    acc = acc + xraw * cw_ref[3:4, :]
    conv = _silu(acc)
    tail = xp_scr[L + 8 - K1:L + 8, :]

    @pl.when(c == nc - 1)
    def _():
        convo_ref[...] = tail

    xp_scr[8 - K1:8, :] = tail

    xs = conv[:, :D_INNER]
    bm = conv[:, D_INNER:D_INNER + SSD_GROUPS * SSD_STATE]
    cm = conv[:, D_INNER + SSD_GROUPS * SSD_STATE:]

    dtv = _softplus(dt_ref[...] + dtb_ref[...])
    d_a = dtv * (-jnp.exp(alog_ref[...]))

    row = lax.broadcasted_iota(I32, (L, LK), 0)
    col = lax.broadcasted_iota(I32, (L, LK), 1)
    causal = col <= row
    tril = jnp.where(causal, 1.0, 0.0).astype(BF16)
    cumc = sum(_dot(tril, p) for p in _split3(pad_rows(d_a)))
    ir = lax.broadcasted_iota(I32, (LANES, LANES), 0)
    ic = lax.broadcasted_iota(I32, (LANES, LANES), 1)
    ident = jnp.where(ir == ic, 1.0, 0.0).astype(BF16)
    cumr = sum(_dot_nt(ident, p) for p in _split3(pad_rows(cumc)))

    ecum = jnp.exp(cumc)
    last = cumc[L - 1:L, :]
    ws = jnp.exp(last - cumc)
    elc = jnp.exp(cumr[:, L - 1:L])
    e_mat = e_ref[...]
    dt_x = _dot(dtv.astype(BF16), e_mat)
    ecum_x = _dot(ecum.astype(BF16), e_mat)
    ws_x = _dot(ws.astype(BF16), e_mat)

    xdt = xs * dt_x
    xdt_b = xdt.astype(BF16)
    xw_p = pad_rows((xdt * ws_x).astype(BF16))
    lo_half = lax.broadcasted_iota(I32, (LK, LANES), 1) < SSD_HEAD_DIM
    zero_b = jnp.zeros((LK, LANES), BF16)
    dskip = dskip_ref[...]
    neg_inf = jnp.float32(-jnp.inf)

    ys = []
    for g in range(SSD_GROUPS):
        bc_g = bm[:, g * SSD_STATE:(g + 1) * SSD_STATE].astype(BF16)
        cc_g = cm[:, g * SSD_STATE:(g + 1) * SSD_STATE].astype(BF16)
        bc_p = pad_rows(bc_g)
        cb = _dot_nt(cc_g, bc_p)
        h_g = h_scr[g * GROUP_INNER:(g + 1) * GROUP_INNER, :]
        ys_g = _dot_nt(cc_g, h_g.astype(BF16))
        for q in range(HPG // 2):
            ha = g * HPG + 2 * q
            c0 = (g * (HPG // 2) + q) * LANES
            dec = []
            for hh in (ha, ha + 1):
                seg = cumc[:, hh:hh + 1] - cumr[hh:hh + 1, :]
                dec.append(cb * jnp.exp(jnp.where(causal, seg, neg_inf)))
            m_cat = jnp.concatenate(dec, axis=1).astype(BF16)
            x_p = pad_rows(xdt_b[:, c0:c0 + LANES])
            xbd = jnp.concatenate([jnp.where(lo_half, x_p, zero_b), jnp.where(lo_half, zero_b, x_p)], axis=0)
            y_p = _dot(m_cat, xbd)
            y_p = y_p + ys_g[:, q * LANES:(q + 1) * LANES] * ecum_x[:, c0:c0 + LANES]
            y_p = y_p + dskip[:, c0:c0 + LANES] * xs[:, c0:c0 + LANES]
            ys.append(y_p)
        upd = _dot_tn(xw_p[:, g * GROUP_INNER:(g + 1) * GROUP_INNER], bc_p)
        scaled = [h_g[j * SSD_HEAD_DIM:(j + 1) * SSD_HEAD_DIM, :] * elc[g * HPG + j:g * HPG + j + 1, :]
                  for j in range(HPG)]
        h_scr[g * GROUP_INNER:(g + 1) * GROUP_INNER, :] = jnp.concatenate(scaled, axis=0) + upd

    y = jnp.concatenate(ys, axis=1)
    gated = y * _silu(z_ref[...].astype(F32))
    outs = []
    for g in range(SSD_GROUPS):
        gg = gated[:, g * GROUP_INNER:(g + 1) * GROUP_INNER]
        outs.append(gg * lax.rsqrt(jnp.mean(gg * gg, axis=-1, keepdims=True) + RMS_EPS))
    y_ref[...] = (jnp.concatenate(outs, axis=1) * nw_ref[...]).astype(y_ref.dtype)

    @pl.when(c == nc - 1)
    def _():
        sso_ref[...] = h_scr[...].reshape(SSD_HEADS, SSD_HEAD_DIM, SSD_STATE)


def _ssd_core(xbc, z, dt, prm, bsz, t, L, y_dtype, init=None):
    nc = t // L
    m = bsz * t
    has_init = init is not None
    tok = lambda n: pl.BlockSpec((L, n), lambda b, c: (b * nc + c, 0))
    full = lambda a: pl.BlockSpec(a.shape, lambda b, c: (0, 0))
    params = [prm["conv_w"], prm["conv_b"], prm["dt_bias"], prm["a_log"], prm["d_skip"], prm["norm_w"], prm["expand"]]
    in_specs = [tok(CONV_DIM), tok(D_INNER), tok(LANES)] + [full(a) for a in params]
    args = [xbc, z, dt] + params
    if has_init:
        conv0, ssm0, layer = init
        in_specs += [pl.BlockSpec((None, None, SSD_CONV - 1, CONV_DIM), lambda b, c: (layer, b, 0, 0)),
                     pl.BlockSpec((None, None, SSD_HEADS, SSD_HEAD_DIM, SSD_STATE), lambda b, c: (layer, b, 0, 0, 0))]
        args += [conv0, ssm0]
    return pl.pallas_call(
        functools.partial(_ssd_kernel, L=L, has_init=has_init),
        grid=(bsz, nc),
        in_specs=in_specs,
        out_specs=[tok(D_INNER),
                   pl.BlockSpec((None, SSD_CONV - 1, CONV_DIM), lambda b, c: (b, 0, 0)),
                   pl.BlockSpec((None, SSD_HEADS, SSD_HEAD_DIM, SSD_STATE), lambda b, c: (b, 0, 0, 0))],
        out_shape=[jax.ShapeDtypeStruct((m, D_INNER), y_dtype),
                   jax.ShapeDtypeStruct((bsz, SSD_CONV - 1, CONV_DIM), F32),
                   jax.ShapeDtypeStruct((bsz, SSD_HEADS, SSD_HEAD_DIM, SSD_STATE), F32)],
        scratch_shapes=[pltpu.VMEM((L + 8, CONV_DIM), F32), pltpu.VMEM((D_INNER, SSD_STATE), F32)],
        compiler_params=_cparams("parallel", "arbitrary"),
        name="ssd_core_init" if has_init else "ssd_core",
    )(*args)


def _proj_ln_kernel(y_ref, w_ref, xr_ref, g_ref, b_ref, o_ref):
    h = _dot(y_ref[...].astype(BF16), w_ref[...])
    o_ref[...] = _layer_norm(DEEPNORM_ALPHA * xr_ref[...] + h, g_ref[...], b_ref[...])


def _proj_ln(y, w, xres, g, b, tm):
    m, k = y.shape
    full = lambda a: pl.BlockSpec(a.shape, lambda i: (0, 0))
    row = lambda n: pl.BlockSpec((tm, n), lambda i: (i, 0))
    return pl.pallas_call(
        _proj_ln_kernel,
        grid=(m // tm,),
        in_specs=[row(k), full(w), row(D_MODEL), full(g), full(b)],
        out_specs=row(D_MODEL),
        out_shape=jax.ShapeDtypeStruct((m, D_MODEL), F32),
        compiler_params=_cparams("parallel"),
        name="proj_ln",
    )(y, w, xres, g, b)


def _mla_down_kernel(x_ref, w_ref, qn_ref, kvn_ref, cos_ref, sin_ref, cq_ref, ckv_ref, kr_ref, kcat_ref):
    c = _dot(x_ref[...].astype(BF16), w_ref[...])
    cq = _rms_norm(c[:, :Q_LORA], qn_ref[...])
    ckv = _rms_norm(c[:, Q_LORA:Q_LORA + KV_LORA], kvn_ref[...])
    kr = _rope128(c[:, Q_LORA + KV_LORA:], cos_ref[...], sin_ref[...])
    cq_ref[...] = cq.astype(BF16)
    ckv_ref[...] = ckv
    kr_ref[...] = kr[:, :QK_ROPE]
    kcat_ref[...] = jnp.concatenate([ckv, kr], axis=1).astype(kcat_ref.dtype)


def _mla_down(x, w, qn, kvn, cos, sin, kcat_dtype, tm):
    m = x.shape[0]
    nper = cos.shape[0] // tm
    full = lambda a: pl.BlockSpec(a.shape, lambda i: (0, 0))
    row = lambda n: pl.BlockSpec((tm, n), lambda i: (i, 0))
    tab = pl.BlockSpec((tm, LANES), lambda i: (i % nper, 0))
    return pl.pallas_call(
        _mla_down_kernel,
        grid=(m // tm,),
        in_specs=[row(D_MODEL), full(w), full(qn), full(kvn), tab, tab],
        out_specs=[row(Q_LORA), row(KV_LORA), row(QK_ROPE), row(KCAT)],
        out_shape=[jax.ShapeDtypeStruct((m, Q_LORA), BF16),
                   jax.ShapeDtypeStruct((m, KV_LORA), F32),
                   jax.ShapeDtypeStruct((m, QK_ROPE), F32),
                   jax.ShapeDtypeStruct((m, KCAT), kcat_dtype)],
        compiler_params=_cparams("parallel"),
        name="mla_down",
    )(x, w, qn, kvn, cos, sin)


def _mla_q_kernel(cq_ref, wn_ref, wr_ref, wuk_ref, cos_ref, sin_ref, q_ref):
    cq = cq_ref[...]
    qn = _dot(cq, wn_ref[...]).astype(BF16)
    qr = _dot(cq, wr_ref[...])
    cos = cos_ref[...]
    sin = sin_ref[...]
    for h in range(MLA_HEADS):
        ql = _dot(qn[:, h * QK_NOPE:(h + 1) * QK_NOPE], wuk_ref[h]) * ATTN_SCALE
        rp = _rope128(qr[:, h * LANES:(h + 1) * LANES], cos, sin) * ATTN_SCALE
        q_ref[:, h * KCAT:h * KCAT + KV_LORA] = ql.astype(q_ref.dtype)
        q_ref[:, h * KCAT + KV_LORA:(h + 1) * KCAT] = rp.astype(q_ref.dtype)


def _mla_q(cq, wn, wr, wuk, cos, sin, q_dtype, tm):
    m = cq.shape[0]
    nper = cos.shape[0] // tm
    full2 = lambda a: pl.BlockSpec(a.shape, lambda i: (0, 0))
    tab = pl.BlockSpec((tm, LANES), lambda i: (i % nper, 0))
    return pl.pallas_call(
        _mla_q_kernel,
        grid=(m // tm,),
        in_specs=[pl.BlockSpec((tm, Q_LORA), lambda i: (i, 0)), full2(wn), full2(wr),
                  pl.BlockSpec(wuk.shape, lambda i: (0, 0, 0)), tab, tab],
        out_specs=pl.BlockSpec((tm, MLA_HEADS * KCAT), lambda i: (i, 0)),
        out_shape=jax.ShapeDtypeStruct((m, MLA_HEADS * KCAT), q_dtype),
        compiler_params=_cparams("parallel"),
        name="mla_q",
    )(cq, wn, wr, wuk, cos, sin)


def _softmax_step(s, v, m_scr, l_scr, acc_scr):
    m_prev = m_scr[...]
    m_new = jnp.maximum(m_prev, jnp.max(s, axis=-1, keepdims=True))
    corr = jnp.exp(m_prev - m_new)
    p = jnp.exp(s - m_new)
    l_scr[...] = corr * l_scr[...] + jnp.sum(p, axis=-1, keepdims=True)
    acc_scr[...] = corr * acc_scr[...] + _dot(p.astype(BF16), v)
    m_scr[...] = m_new


def _attn_prompt_kernel(qi_ref, kj_ref, q_ref, k_ref, o_ref, qs_scr, m_scr, l_scr, acc_scr, *, tq, tk):
    p = pl.program_id(1)
    i = qi_ref[p]
    j = kj_ref[p]
    rows = MLA_HEADS * tq

    @pl.when(j == 0)
    def _():
        for h in range(MLA_HEADS):
            qs_scr[h * tq:(h + 1) * tq, :] = q_ref[:, h * KCAT:(h + 1) * KCAT]
        m_scr[...] = jnp.full((rows, 1), -jnp.inf, F32)
        l_scr[...] = jnp.zeros((rows, 1), F32)
        acc_scr[...] = jnp.zeros((rows, KV_LORA), F32)

    k = k_ref[...]
    s = _dot_nt(qs_scr[...], k)
    v = k[:, :KV_LORA]
    crosses = (j + 1) * tk - 1 > i * tq

    @pl.when(crosses)
    def _():
        qpos = (lax.broadcasted_iota(I32, (rows, tk), 0) & (tq - 1)) + i * tq
        kpos = lax.broadcasted_iota(I32, (rows, tk), 1) + j * tk
        _softmax_step(jnp.where(kpos <= qpos, s, -jnp.inf), v, m_scr, l_scr, acc_scr)

    @pl.when(jnp.logical_not(crosses))
    def _():
        _softmax_step(s, v, m_scr, l_scr, acc_scr)

    @pl.when(j == ((i + 1) * tq - 1) // tk)
    def _():
        inv = 1.0 / l_scr[...]
        for h in range(MLA_HEADS):
            o_ref[:, h * KV_LORA:(h + 1) * KV_LORA] = (
                acc_scr[h * tq:(h + 1) * tq, :] * inv[h * tq:(h + 1) * tq, :]).astype(o_ref.dtype)


def _attn_prompt(qcat, kcat, bsz, t, tq, tk):
    nq, nk = t // tq, t // tk
    pairs = [(i, j) for i in range(nq) for j in range(((i + 1) * tq - 1) // tk + 1)]
    qi = jnp.asarray([p[0] for p in pairs], I32)
    kj = jnp.asarray([p[1] for p in pairs], I32)
    rows = MLA_HEADS * tq
    grid_spec = pltpu.PrefetchScalarGridSpec(
        num_scalar_prefetch=2,
        grid=(bsz, len(pairs)),
        in_specs=[pl.BlockSpec((tq, MLA_HEADS * KCAT), lambda b, p, qi, kj: (b * nq + qi[p], 0)),
                  pl.BlockSpec((tk, KCAT), lambda b, p, qi, kj: (b * nk + kj[p], 0))],
        out_specs=pl.BlockSpec((tq, MLA_HEADS * KV_LORA), lambda b, p, qi, kj: (b * nq + qi[p], 0)),
        scratch_shapes=[pltpu.VMEM((rows, KCAT), BF16), pltpu.VMEM((rows, 1), F32),
                        pltpu.VMEM((rows, 1), F32), pltpu.VMEM((rows, KV_LORA), F32)],
    )
    return pl.pallas_call(
        functools.partial(_attn_prompt_kernel, tq=tq, tk=tk),
        grid_spec=grid_spec,
        out_shape=jax.ShapeDtypeStruct((bsz * t, MLA_HEADS * KV_LORA), BF16),
        compiler_params=_cparams("parallel", "arbitrary"),
        name="attn_prompt",
    )(qi, kj, qcat, kcat)


def _attn_sample_kernel(pt_ref, q_ref, kn_ref, *rest, npages, t):
    lat_refs = rest[:npages]
    kr_refs = rest[npages:2 * npages]
    o_ref = rest[2 * npages]
    qs_scr, m_scr, l_scr, acc_scr = rest[2 * npages + 1:]
    step = pl.program_id(1)
    rows = MLA_HEADS * t

    @pl.when(step == 0)
    def _():
        for h in range(MLA_HEADS):
            qs_scr[h * t:(h + 1) * t, :] = q_ref[:, h * KCAT:(h + 1) * KCAT]
        m_scr[...] = jnp.full((rows, 1), -jnp.inf, F32)
        l_scr[...] = jnp.zeros((rows, 1), F32)
        acc_scr[...] = jnp.zeros((rows, KV_LORA), F32)

    qs = qs_scr[...].astype(BF16)
    klat = jnp.concatenate([r[...] for r in lat_refs], axis=0).astype(BF16)
    krope = jnp.concatenate([r[...] for r in kr_refs], axis=0).astype(BF16)
    s = _dot_nt(qs[:, :KV_LORA], klat) + _dot_nt(qs[:, KV_LORA:KV_LORA + QK_ROPE], krope)
    _softmax_step(s, klat, m_scr, l_scr, acc_scr)

    @pl.when(step == pl.num_programs(1) - 1)
    def _():
        kn = jnp.concatenate([kn_ref[...], jnp.zeros((LANES - t, KCAT), F32)], axis=0).astype(BF16)
        sn = _dot_nt(qs, kn)
        qpos = lax.broadcasted_iota(I32, (rows, LANES), 0) & (t - 1)
        kpos = lax.broadcasted_iota(I32, (rows, LANES), 1)
        _softmax_step(jnp.where(kpos <= qpos, sn, -jnp.inf), kn[:, :KV_LORA], m_scr, l_scr, acc_scr)
        inv = 1.0 / l_scr[...]
        for h in range(MLA_HEADS):
            o_ref[:, h * KV_LORA:(h + 1) * KV_LORA] = acc_scr[h * t:(h + 1) * t, :] * inv[h * t:(h + 1) * t, :]


def _attn_sample(qcat, kcat, cache_lat, cache_kr, page_table, layer, bsz, t, npages):
    n_pages = page_table.shape[1]
    nsteps = n_pages // npages

    def page_map(b, s, pt, *, i):
        return (layer, pt[b, s * npages + i], 0, 0)

    lat_specs = [pl.BlockSpec((None, None, PAGE_SIZE, KV_LORA), functools.partial(page_map, i=i)) for i in range(npages)]
    kr_specs = [pl.BlockSpec((None, None, PAGE_SIZE, QK_ROPE), functools.partial(page_map, i=i)) for i in range(npages)]
    rows = MLA_HEADS * t
    grid_spec = pltpu.PrefetchScalarGridSpec(
        num_scalar_prefetch=1,
        grid=(bsz, nsteps),
        in_specs=[pl.BlockSpec((None, t, MLA_HEADS * KCAT), lambda b, s, pt: (b, 0, 0)),
                  pl.BlockSpec((None, t, KCAT), lambda b, s, pt: (b, 0, 0))] + lat_specs + kr_specs,
        out_specs=pl.BlockSpec((None, t, MLA_HEADS * KV_LORA), lambda b, s, pt: (b, 0, 0)),
        scratch_shapes=[pltpu.VMEM((rows, KCAT), F32), pltpu.VMEM((rows, 1), F32),
                        pltpu.VMEM((rows, 1), F32), pltpu.VMEM((rows, KV_LORA), F32)],
    )
    out = pl.pallas_call(
        functools.partial(_attn_sample_kernel, npages=npages, t=t),
        grid_spec=grid_spec,
        out_shape=jax.ShapeDtypeStruct((bsz, t, MLA_HEADS * KV_LORA), F32),
        compiler_params=_cparams("parallel", "arbitrary"),
        name="attn_sample",
    )(page_table, qcat.reshape(bsz, t, -1), kcat.reshape(bsz, t, -1),
      *([cache_lat] * npages), *([cache_kr] * npages))
    return out.reshape(bsz * t, -1)


def _mla_out_kernel(o_ref, wuv_ref, wo_ref, xr_ref, g_ref, b_ref, out_ref):
    o = o_ref[...].astype(BF16)
    parts = [_dot(o[:, h * KV_LORA:(h + 1) * KV_LORA], wuv_ref[h]).astype(BF16) for h in range(MLA_HEADS)]
    h = _dot(jnp.concatenate(parts, axis=1), wo_ref[...])
    out_ref[...] = _layer_norm(DEEPNORM_ALPHA * xr_ref[...] + h, g_ref[...], b_ref[...])


def _mla_out(o, wuv, wo, xres, g, b, tm):
    m = o.shape[0]
    full = lambda a: pl.BlockSpec(a.shape, lambda i: (0,) * a.ndim)
    row = lambda n: pl.BlockSpec((tm, n), lambda i: (i, 0))
    return pl.pallas_call(
        _mla_out_kernel,
        grid=(m // tm,),
        in_specs=[row(MLA_HEADS * KV_LORA), full(wuv), full(wo), row(D_MODEL), full(g), full(b)],
        out_specs=row(D_MODEL),
        out_shape=jax.ShapeDtypeStruct((m, D_MODEL), F32),
        compiler_params=_cparams("parallel"),
        name="mla_out",
    )(o, wuv, wo, xres, g, b)


def _router_kernel(x_ref, rh_ref, rl_ref, bias_ref, bucket_ref, rank_ref, counts_ref, cnt_scr, *, tm):
    i = pl.program_id(0)

    @pl.when(i == 0)
    def _():
        cnt_scr[...] = jnp.zeros((N_BUCKETS, 1), F32)

    xh, xm = _split2(x_ref[...])
    rh = rh_ref[...]
    logits = _dot_nt(rh, xh) + _dot_nt(rh, xm) + _dot_nt(rl_ref[...], xh)
    sel = jax.nn.sigmoid(logits) + bias_ref[...]
    sub = lax.broadcasted_iota(I32, (EXPERTS_PER_GROUP, tm), 0)
    for g in range(MOE_GROUPS):
        sg = sel[g * EXPERTS_PER_GROUP:(g + 1) * EXPERTS_PER_GROUP, :]
        m1 = jnp.max(sg, axis=0, keepdims=True)
        i1 = jnp.min(jnp.where(sg == m1, sub, EXPERTS_PER_GROUP), axis=0, keepdims=True)
        sg2 = jnp.where(sub == i1, -jnp.inf, sg)
        m2 = jnp.max(sg2, axis=0, keepdims=True)
        i2 = jnp.min(jnp.where(sg2 == m2, sub, EXPERTS_PER_GROUP), axis=0, keepdims=True)
        score = m1 + m2
        if g == 0:
            best, bg, ba, bb = score, jnp.zeros((1, tm), I32), i1, i2
        else:
            better = score > best
            best = jnp.where(better, score, best)
            bg = jnp.where(better, g, bg)
            ba = jnp.where(better, i1, ba)
            bb = jnp.where(better, i2, bb)
    bucket = bg * 64 + jnp.minimum(ba, bb) * 8 + jnp.maximum(ba, bb)

    onehot = jnp.where(lax.broadcasted_iota(I32, (N_BUCKETS, tm), 0) == bucket, 1.0, 0.0)
    before = lax.broadcasted_iota(I32, (tm, tm), 0) < lax.broadcasted_iota(I32, (tm, tm), 1)
    earlier = _dot(onehot.astype(BF16), jnp.where(before, 1.0, 0.0).astype(BF16))
    rank = jnp.sum(onehot * (earlier + cnt_scr[...]), axis=0, keepdims=True)
    cnt_scr[...] = cnt_scr[...] + jnp.sum(onehot, axis=1, keepdims=True)
    bucket_ref[...] = bucket
    rank_ref[...] = rank.astype(I32)

    @pl.when(i == pl.num_programs(0) - 1)
    def _():
        counts_ref[...] = cnt_scr[...]


def _router(x, rh, rl, bias, tm):
    m = x.shape[0]
    full = lambda a: pl.BlockSpec(a.shape, lambda i: (0, 0))
    lane_row = pl.BlockSpec((1, tm), lambda i: (0, i))
    return pl.pallas_call(
        functools.partial(_router_kernel, tm=tm),
        grid=(m // tm,),
        in_specs=[pl.BlockSpec((tm, D_MODEL), lambda i: (i, 0)), full(rh), full(rl), full(bias)],
        out_specs=[lane_row, lane_row, pl.BlockSpec((N_BUCKETS, 1), lambda i: (0, 0))],
        out_shape=[jax.ShapeDtypeStruct((1, m), I32), jax.ShapeDtypeStruct((1, m), I32),
                   jax.ShapeDtypeStruct((N_BUCKETS, 1), F32)],
        scratch_shapes=[pltpu.VMEM((N_BUCKETS, 1), F32)],
        compiler_params=_cparams("arbitrary"),
        name="moe_router",
    )(x, rh, rl, bias)


def _row_copy(src, dst, sem):
    return pltpu.make_async_copy(src, dst, sem)


def _moe_scatter_kernel(pos_ref, x_ref, init_ref, xs_ref, buf, sems, *, tm):
    del init_ref
    i = pl.program_id(0)
    n = pl.num_programs(0)
    slot = i % 2

    def wait_slot(sl):
        def body(r, carry):
            _row_copy(buf.at[sl, pl.ds(0, 1)], xs_ref.at[pl.ds(0, 1)], sems.at[sl]).wait()
            return carry
        lax.fori_loop(0, tm, body, 0)

    @pl.when(i >= 2)
    def _():
        wait_slot(slot)

    buf[slot] = x_ref[...]

    def issue(r, carry):
        _row_copy(buf.at[slot, pl.ds(r, 1)], xs_ref.at[pl.ds(pos_ref[0, 0, r], 1)], sems.at[slot]).start()
        return carry
    lax.fori_loop(0, tm, issue, 0)

    @pl.when(i == n - 1)
    def _():
        wait_slot(slot)

    @pl.when(jnp.logical_and(i == n - 1, n >= 2))
    def _():
        wait_slot(1 - slot)


def _moe_scatter(x, pos, n_rows, tm):
    m = x.shape[0]
    return pl.pallas_call(
        functools.partial(_moe_scatter_kernel, tm=tm),
        grid=(m // tm,),
        in_specs=[pl.BlockSpec((1, 1, tm), lambda i: (i, 0, 0), memory_space=pltpu.SMEM),
                  pl.BlockSpec((tm, D_MODEL), lambda i: (i, 0)),
                  pl.BlockSpec(memory_space=pl.ANY)],
        out_specs=pl.BlockSpec(memory_space=pl.ANY),
        out_shape=jax.ShapeDtypeStruct((n_rows, D_MODEL), F32),
        scratch_shapes=[pltpu.VMEM((2, tm, D_MODEL), F32), pltpu.SemaphoreType.DMA((2,))],
        input_output_aliases={2: 0},
        compiler_params=_cparams("arbitrary"),
        name="moe_scatter",
    )(pos.reshape(m // tm, 1, tm), x, jnp.zeros((n_rows, D_MODEL), F32))


def _moe_ffn_kernel(e0_ref, e1_ref, nu_ref, xs_ref, r0_ref, r1_ref, wg0_ref, wg1_ref, wu0_ref, wu1_ref,
                    wd0_ref, wd1_ref, y_ref):
    del e0_ref, e1_ref
    i = pl.program_id(0)

    @pl.when(i < nu_ref[0])
    def _():
        x = xs_ref[...]
        xb = x.astype(BF16)
        a0 = jax.nn.sigmoid(jnp.sum(x * r0_ref[...], axis=-1, keepdims=True))
        a1 = jax.nn.sigmoid(jnp.sum(x * r1_ref[...], axis=-1, keepdims=True))
        den = a0 + a1
        h0 = _silu(_dot(xb, wg0_ref[...])) * _dot(xb, wu0_ref[...]) * (a0 / den)
        h1 = _silu(_dot(xb, wg1_ref[...])) * _dot(xb, wu1_ref[...]) * (a1 / den)
        y_ref[...] = _dot(h0.astype(BF16), wd0_ref[...]) + _dot(h1.astype(BF16), wd1_ref[...])

    @pl.when(i >= nu_ref[0])
    def _():
        y_ref[...] = jnp.zeros(y_ref.shape, F32)


def _moe_ffn(xs, e0, e1, n_used, router_t, wg, wu, wd):
    n_tiles = xs.shape[0] // MOE_TILE
    by0 = lambda i, e0, e1, nu: (e0[i], 0, 0)
    by1 = lambda i, e0, e1, nu: (e1[i], 0, 0)
    spec = lambda shape, fn: pl.BlockSpec((None,) + shape, fn)
    grid_spec = pltpu.PrefetchScalarGridSpec(
        num_scalar_prefetch=3,
        grid=(n_tiles,),
        in_specs=[pl.BlockSpec((MOE_TILE, D_MODEL), lambda i, e0, e1, nu: (jnp.minimum(i, nu[0] - 1), 0)),
                  spec((1, D_MODEL), by0), spec((1, D_MODEL), by1),
                  spec((D_MODEL, EXPERT_FF), by0), spec((D_MODEL, EXPERT_FF), by1),
                  spec((D_MODEL, EXPERT_FF), by0), spec((D_MODEL, EXPERT_FF), by1),
                  spec((EXPERT_FF, D_MODEL), by0), spec((EXPERT_FF, D_MODEL), by1)],
        out_specs=pl.BlockSpec((MOE_TILE, D_MODEL), lambda i, e0, e1, nu: (i, 0)),
    )
    return pl.pallas_call(
        _moe_ffn_kernel,
        grid_spec=grid_spec,
        out_shape=jax.ShapeDtypeStruct(xs.shape, F32),
        compiler_params=_cparams("arbitrary"),
        name="moe_ffn",
    )(e0, e1, n_used, xs, router_t, router_t, wg, wg, wu, wu, wd, wd)


def _moe_combine_kernel(pos_ref, posn_ref, ys_ref, xr_ref, g_ref, b_ref, o_ref, buf, sems, *, tm):
    i = pl.program_id(0)
    n = pl.num_programs(0)
    slot = i % 2

    def issue(pref, sl):
        def body(r, carry):
            _row_copy(ys_ref.at[pl.ds(pref[0, 0, r], 1)], buf.at[sl, pl.ds(r, 1)], sems.at[sl]).start()
            return carry
        lax.fori_loop(0, tm, body, 0)

    @pl.when(i == 0)
    def _():
        issue(pos_ref, 0)

    @pl.when(i + 1 < n)
    def _():
        issue(posn_ref, 1 - slot)

    def wait_body(r, carry):
        _row_copy(ys_ref.at[pl.ds(0, 1)], buf.at[slot, pl.ds(0, 1)], sems.at[slot]).wait()
        return carry
    lax.fori_loop(0, tm, wait_body, 0)

    o_ref[...] = _layer_norm(DEEPNORM_ALPHA * xr_ref[...] + buf[slot], g_ref[...], b_ref[...])


def _moe_combine(ys, pos, xres, g, b, tm):
    m = xres.shape[0]
    n = m // tm
    full = lambda a: pl.BlockSpec(a.shape, lambda i: (0, 0))
    pos3 = pos.reshape(n, 1, tm)
    return pl.pallas_call(
        functools.partial(_moe_combine_kernel, tm=tm),
        grid=(n,),
        in_specs=[pl.BlockSpec((1, 1, tm), lambda i: (i, 0, 0), memory_space=pltpu.SMEM),
                  pl.BlockSpec((1, 1, tm), lambda i: (jnp.minimum(i + 1, n - 1), 0, 0), memory_space=pltpu.SMEM),
                  pl.BlockSpec(memory_space=pl.ANY),
                  pl.BlockSpec((tm, D_MODEL), lambda i: (i, 0)), full(g), full(b)],
        out_specs=pl.BlockSpec((tm, D_MODEL), lambda i: (i, 0)),
        out_shape=jax.ShapeDtypeStruct((m, D_MODEL), F32),
        scratch_shapes=[pltpu.VMEM((2, tm, D_MODEL), F32), pltpu.SemaphoreType.DMA((2,))],
        compiler_params=_cparams("arbitrary"),
        name="moe_combine",
    )(pos3, pos3, ys, xres, g, b)


def _hier_moe_ln(x, prm, g, b, tm_route, tm_dma):
    m = x.shape[0]
    bucket, rank, counts = _router(x, prm["router_hi"], prm["router_lo"], prm["router_bias"], tm_route)
    counts = counts[:, 0].astype(I32)
    tiles_per = (counts + MOE_TILE - 1) // MOE_TILE
    tile_end = jnp.cumsum(tiles_per)
    n_used = tile_end[-1:]
    n_tiles = m // MOE_TILE + MAX_PAIRS
    pos = ((tile_end - tiles_per) * MOE_TILE)[bucket[0]] + rank[0]
    tile_bucket = jnp.searchsorted(tile_end, jnp.minimum(jnp.arange(n_tiles, dtype=I32), n_used - 1), side="right")
    tile_bucket = tile_bucket.astype(I32)
    grp = tile_bucket // 64
    e0 = grp * EXPERTS_PER_GROUP + (tile_bucket // 8) % 8
    e1 = grp * EXPERTS_PER_GROUP + tile_bucket % 8
    xs = _moe_scatter(x, pos, n_tiles * MOE_TILE, tm_dma)
    ys = _moe_ffn(xs, e0, e1, n_used.astype(I32), prm["router_t"], prm["w_gate"], prm["w_up"], prm["w_down"])
    return _moe_combine(ys, pos, x, g, b, tm_dma)


def _pad_cols(a, n):
    return jnp.pad(a, ((0, 0), (0, n - a.shape[1])))


def _ssd_params(w_in, conv_w, conv_b, dt_bias, a_log, d_skip, norm_w, w_out):
    wdt = _pad_cols(w_in[:, D_INNER + CONV_DIM:], LANES)
    wdh = wdt.astype(BF16)
    head_of_col = jnp.arange(D_INNER, dtype=I32) // SSD_HEAD_DIM
    expand = (jnp.arange(LANES, dtype=I32)[:, None] == head_of_col[None, :]).astype(BF16)
    return {
        "wz": w_in[:, :D_INNER].astype(BF16),
        "wx": w_in[:, D_INNER:D_INNER + CONV_DIM].astype(BF16),
        "wdh": wdh,
        "wdl": (wdt - wdh.astype(F32)).astype(BF16),
        "conv_w": conv_w,
        "conv_b": conv_b[None, :],
        "dt_bias": _pad_cols(dt_bias[None, :], LANES),
        "a_log": _pad_cols(a_log[None, :], LANES),
        "d_skip": jnp.repeat(d_skip, SSD_HEAD_DIM)[None, :],
        "norm_w": norm_w[None, :],
        "expand": expand,
        "w_out": w_out.astype(BF16),
    }


def _mla_params(w_dqkv, q_norm, kv_norm, w_uq, w_uk, w_uv, w_o):
    wr = jnp.pad(w_uq[:, :, QK_NOPE:], ((0, 0), (0, 0), (0, LANES - QK_ROPE)))
    return {
        "w_down": _pad_cols(w_dqkv, DCOLS).astype(BF16),
        "q_norm": q_norm[None, :],
        "kv_norm": kv_norm[None, :],
        "w_qn": w_uq[:, :, :QK_NOPE].reshape(Q_LORA, MLA_HEADS * QK_NOPE).astype(BF16),
        "w_qr": wr.reshape(Q_LORA, MLA_HEADS * LANES).astype(BF16),
        "w_uk_t": jnp.transpose(w_uk, (1, 2, 0)).astype(BF16),
        "w_uv": jnp.transpose(w_uv, (1, 0, 2)).astype(BF16),
        "w_o": w_o.astype(BF16),
    }


def _moe_params(router, router_bias, w_gate, w_up, w_down):
    rt = router.T
    rh = rt.astype(BF16)
    return {
        "router_hi": rh,
        "router_lo": (rt - rh.astype(F32)).astype(BF16),
        "router_t": rt[:, None, :],
        "router_bias": router_bias[:, None],
        "w_gate": w_gate.astype(BF16),
        "w_up": w_up.astype(BF16),
        "w_down": w_down.astype(BF16),
    }


def _rope_tables(pos):
    half = QK_ROPE // 2
    inv_freq = ROPE_THETA ** (-jnp.arange(half, dtype=F32) / half)
    ang = pos.astype(F32)[:, None] * inv_freq[None, :]
    cos, sin = jnp.cos(ang), jnp.sin(ang)
    zeros = jnp.zeros((pos.shape[0], LANES - QK_ROPE), F32)
    return jnp.concatenate([cos, cos, zeros], axis=1), jnp.concatenate([-sin, sin, zeros], axis=1)


def _tile(m, want):
    return min(m, want)


def kernel(x_prompt, x_sample, cache_mla_latent, cache_mla_krope, state_ssd, state_conv, page_table,
           ln_mix_g, ln_mix_b, ln_ffn_g, ln_ffn_b,
           ssd_w_in, ssd_conv_w, ssd_conv_b, ssd_dt_bias, ssd_a_log, ssd_d_skip, ssd_norm_w, ssd_w_out,
           mla_w_dqkv, mla_q_norm, mla_kv_norm, mla_w_uq, mla_w_uk, mla_w_uv, mla_w_o,
           moe_router, moe_router_bias, moe_w_gate, moe_w_up, moe_w_down):
    bp, tp, _ = x_prompt.shape
    bs, ts, _ = x_sample.shape
    past_len = page_table.shape[1] * PAGE_SIZE
    xp = x_prompt.reshape(bp * tp, D_MODEL)
    xs = x_sample.reshape(bs * ts, D_MODEL)
    lp = min(SSD_CHUNK, tp)

    cos_p, sin_p = _rope_tables(jnp.arange(tp))
    cos_s, sin_s = _rope_tables(past_len + jnp.arange(ts))
    cos_s = jnp.tile(cos_s, (bs, 1))
    sin_s = jnp.tile(sin_s, (bs, 1))

    p_lat, p_kr, p_ssm, p_conv = [], [], [], []
    s_lat, s_kr, s_ssm, s_conv = [], [], [], []
    for i in range(DEPTH):
        j = i // 2
        g_mix, b_mix = ln_mix_g[i][None, :], ln_mix_b[i][None, :]
        if i % 2 == 0:
            prm = _ssd_params(ssd_w_in[j], ssd_conv_w[j], ssd_conv_b[j], ssd_dt_bias[j], ssd_a_log[j],
                              ssd_d_skip[j], ssd_norm_w[j], ssd_w_out[j])
            z, xbc, dt = _ssd_inproj(xp, prm["wz"], prm["wx"], prm["wdh"], prm["wdl"], BF16, _tile(xp.shape[0], 256))
            y, cp, sp = _ssd_core(xbc, z, dt, prm, bp, tp, lp, BF16)
            xp = _proj_ln(y, prm["w_out"], xp, g_mix, b_mix, _tile(xp.shape[0], 512))
            z, xbc, dt = _ssd_inproj(xs, prm["wz"], prm["wx"], prm["wdh"], prm["wdl"], F32, _tile(xs.shape[0], 256))
            y, cs, ss = _ssd_core(xbc, z, dt, prm, bs, ts, ts, F32, init=(state_conv, state_ssd, j))
            xs = _proj_ln(y, prm["w_out"], xs, g_mix, b_mix, _tile(xs.shape[0], 512))
            p_conv.append(cp)
            p_ssm.append(sp)
            s_conv.append(cs)
            s_ssm.append(ss)
        else:
            prm = _mla_params(mla_w_dqkv[j], mla_q_norm[j], mla_kv_norm[j], mla_w_uq[j], mla_w_uk[j],
                              mla_w_uv[j], mla_w_o[j])
            tm = _tile(tp, 512)
            cq, lat, kr, kcat = _mla_down(xp, prm["w_down"], prm["q_norm"], prm["kv_norm"], cos_p, sin_p, BF16, tm)
            qcat = _mla_q(cq, prm["w_qn"], prm["w_qr"], prm["w_uk_t"], cos_p, sin_p, BF16, tm)
            o = _attn_prompt(qcat, kcat, bp, tp, tm, tm)
            xp = _mla_out(o, prm["w_uv"], prm["w_o"], xp, g_mix, b_mix, tm)
            p_lat.append(lat.reshape(bp, tp, KV_LORA))
            p_kr.append(kr.reshape(bp, tp, QK_ROPE))
            tm = _tile(xs.shape[0], 512)
            cq, lat, kr, kcat = _mla_down(xs, prm["w_down"], prm["q_norm"], prm["kv_norm"], cos_s, sin_s, F32, tm)
            qcat = _mla_q(cq, prm["w_qn"], prm["w_qr"], prm["w_uk_t"], cos_s, sin_s, F32, tm)
            o = _attn_sample(qcat, kcat, cache_mla_latent, cache_mla_krope, page_table, j, bs, ts,
                             min(16, page_table.shape[1]))
            xs = _mla_out(o, prm["w_uv"], prm["w_o"], xs, g_mix, b_mix, tm)
            s_lat.append(lat.reshape(bs, ts, KV_LORA))
            s_kr.append(kr.reshape(bs, ts, QK_ROPE))
        mprm = _moe_params(moe_router[i], moe_router_bias[i], moe_w_gate[i], moe_w_up[i], moe_w_down[i])
        g_ffn, b_ffn = ln_ffn_g[i][None, :], ln_ffn_b[i][None, :]
        xp = _hier_moe_ln(xp, mprm, g_ffn, b_ffn, _tile(xp.shape[0], 512), _tile(xp.shape[0], 256))
        xs = _hier_moe_ln(xs, mprm, g_ffn, b_ffn, _tile(xs.shape[0], 512), _tile(xs.shape[0], 256))

    return (xp.reshape(bp, tp, D_MODEL), xs.reshape(bs, ts, D_MODEL),
            jnp.stack(p_lat), jnp.stack(p_kr), jnp.stack(p_ssm), jnp.stack(p_conv),
            jnp.stack(s_lat), jnp.stack(s_kr), jnp.stack(s_ssm), jnp.stack(s_conv))
```

```python
import functools

import jax
import jax.numpy as jnp
from jax import lax
from jax.experimental import pallas as pl
from jax.experimental.pallas import tpu as pltpu

F32 = jnp.float32
BF16 = jnp.bfloat16
I32 = jnp.int32

D_MODEL = 1024
DEPTH = 4
PAGE_SIZE = 128
D_INNER = 2048
SSD_HEAD_DIM = 64
SSD_HEADS = 32
SSD_GROUPS = 4
SSD_STATE = 128
SSD_CONV = 4
SSD_CHUNK = 128
CONV_DIM = D_INNER + 2 * SSD_GROUPS * SSD_STATE
GROUP_INNER = D_INNER // SSD_GROUPS
MLA_HEADS = 8
Q_LORA = 384
KV_LORA = 256
QK_NOPE = 128
QK_ROPE = 64
V_HEAD = 128
ROPE_THETA = 10000.0
MOE_GROUPS = 4
EXPERTS_PER_GROUP = 8
N_EXPERTS = 32
EXPERT_FF = 256
DEEPNORM_ALPHA = (2.0 * DEPTH) ** 0.25
LN_EPS = 1e-5
RMS_EPS = 1e-6
ATTN_SCALE = (QK_NOPE + QK_ROPE) ** -0.5

LANES = 128
KCAT = KV_LORA + LANES
DCOLS = Q_LORA + KV_LORA + LANES
MOE_SUB = 256
MOE_GRAN = 16
MOE_ROWS = 2 * MOE_SUB + N_EXPERTS * MOE_GRAN
MOE_SUBS_PER_STEP = 2
MOE_EXPERTS_PER_STEP = 4
VMEM_LIMIT = 56 * 1024 * 1024


def _cparams(*sem):
    return pltpu.CompilerParams(dimension_semantics=sem, vmem_limit_bytes=VMEM_LIMIT)


def _dot(a, b):
    return jnp.dot(a, b, preferred_element_type=F32)


def _dot_nt(a, b):
    return lax.dot_general(a, b, (((1,), (1,)), ((), ())), preferred_element_type=F32)


def _dot_tn(a, b):
    return lax.dot_general(a, b, (((0,), (0,)), ((), ())), preferred_element_type=F32)


def _split2(x):
    hi = x.astype(BF16)
    return hi, (x - hi.astype(F32)).astype(BF16)


def _split3(x):
    hi = x.astype(BF16)
    r = x - hi.astype(F32)
    mid = r.astype(BF16)
    return hi, mid, (r - mid.astype(F32)).astype(BF16)


def _silu(x):
    return x * jax.nn.sigmoid(x)


def _softplus(x):
    return jnp.maximum(x, 0.0) + jnp.log1p(jnp.exp(-jnp.abs(x)))


def _layer_norm(v, g, b):
    mu = jnp.mean(v, axis=-1, keepdims=True)
    d = v - mu
    var = jnp.mean(d * d, axis=-1, keepdims=True)
    return d * lax.rsqrt(var + LN_EPS) * g + b


def _rms_norm(v, g):
    return v * lax.rsqrt(jnp.mean(v * v, axis=-1, keepdims=True) + RMS_EPS) * g


def _rope128(x, cos, sin):
    lane = lax.broadcasted_iota(I32, x.shape, 1)
    partner = jnp.where(lane < QK_ROPE // 2, pltpu.roll(x, LANES - QK_ROPE // 2, 1), pltpu.roll(x, QK_ROPE // 2, 1))
    return x * cos + partner * sin


def _ssd_inproj_kernel(x_ref, wz_ref, wx_ref, wdh_ref, wdl_ref, z_ref, xbc_ref, dt_ref):
    x = x_ref[...]
    xh, xm = _split2(x)
    z_ref[...] = _dot(xh, wz_ref[...]).astype(z_ref.dtype)
    xbc_ref[...] = _dot(xh, wx_ref[...])
    wdh = wdh_ref[...]
    dt_ref[...] = _dot(xh, wdh) + _dot(xm, wdh) + _dot(xh, wdl_ref[...])


def _ssd_inproj(x, wz, wx, wdh, wdl, z_dtype, tm):
    m = x.shape[0]
    full = lambda a: pl.BlockSpec(a.shape, lambda i: (0, 0))
    row = lambda n: pl.BlockSpec((tm, n), lambda i: (i, 0))
    return pl.pallas_call(
        _ssd_inproj_kernel,
        grid=(m // tm,),
        in_specs=[row(D_MODEL), full(wz), full(wx), full(wdh), full(wdl)],
        out_specs=[row(D_INNER), row(CONV_DIM), row(LANES)],
        out_shape=[jax.ShapeDtypeStruct((m, D_INNER), z_dtype),
                   jax.ShapeDtypeStruct((m, CONV_DIM), F32),
                   jax.ShapeDtypeStruct((m, LANES), F32)],
        compiler_params=_cparams("parallel"),
        name="ssd_inproj",
    )(x, wz, wx, wdh, wdl)


def _ssd_kernel(*refs, L, has_init):
    if has_init:
        (xbc_ref, z_ref, dt_ref, cw_ref, cb_ref, dtb_ref, alog_ref, dskip_ref, nw_ref, e_ref,
         conv0_ref, ssm0_ref, y_ref, convo_ref, sso_ref, xp_scr, h_scr) = refs
    else:
        (xbc_ref, z_ref, dt_ref, cw_ref, cb_ref, dtb_ref, alog_ref, dskip_ref, nw_ref, e_ref,
         y_ref, convo_ref, sso_ref, xp_scr, h_scr) = refs
    c = pl.program_id(1)
    nc = pl.num_programs(1)
    LK = max(L, LANES)
    K1 = SSD_CONV - 1
    HPG = SSD_HEADS // SSD_GROUPS

    @pl.when(c == 0)
    def _():
        if has_init:
            xp_scr[8 - K1:8, :] = conv0_ref[...]
            h_scr[...] = ssm0_ref[...].reshape(D_INNER, SSD_STATE)
        else:
            xp_scr[0:8, :] = jnp.zeros((8, CONV_DIM), F32)
            h_scr[...] = jnp.zeros((D_INNER, SSD_STATE), F32)

    def pad_rows(a):
        if L == LK:
            return a
        return jnp.concatenate([a, jnp.zeros((LK - L, a.shape[1]), a.dtype)], axis=0)

    xraw = xbc_ref[...]
    xp_scr[8:8 + L, :] = xraw
    acc = cb_ref[...] + xp_scr[5:5 + L, :] * cw_ref[0:1, :]
    acc = acc + xp_scr[6:6 + L, :] * cw_ref[1:2, :]
    acc = acc + xp_scr[7:7 + L, :] * cw_ref[2:3, :]
    acc = acc + xraw * cw_ref[3:4, :]
    conv = _silu(acc)
    tail = xp_scr[L + 8 - K1:L + 8, :]

    @pl.when(c == nc - 1)
    def _():
        convo_ref[...] = tail

    xp_scr[8 - K1:8, :] = tail

    xs = conv[:, :D_INNER]
    bm = conv[:, D_INNER:D_INNER + SSD_GROUPS * SSD_STATE]
    cm = conv[:, D_INNER + SSD_GROUPS * SSD_STATE:]

    dtv = _softplus(dt_ref[...] + dtb_ref[...])
    d_a = dtv * (-jnp.exp(alog_ref[...]))

    row = lax.broadcasted_iota(I32, (L, LK), 0)
    col = lax.broadcasted_iota(I32, (L, LK), 1)
    causal = col <= row
    tril = jnp.where(causal, 1.0, 0.0).astype(BF16)
    cumc = sum(_dot(tril, p) for p in _split3(pad_rows(d_a)))
    ir = lax.broadcasted_iota(I32, (LANES, LANES), 0)
    ic = lax.broadcasted_iota(I32, (LANES, LANES), 1)
    ident = jnp.where(ir == ic, 1.0, 0.0).astype(BF16)
    cumr = sum(_dot_nt(ident, p) for p in _split3(pad_rows(cumc)))

    ecum = jnp.exp(cumc)
    last = cumc[L - 1:L, :]
    ws = jnp.exp(last - cumc)
    elc = jnp.exp(cumr[:, L - 1:L])
    e_mat = e_ref[...]
    dt_x = _dot(dtv.astype(BF16), e_mat)
    ecum_x = _dot(ecum.astype(BF16), e_mat)
    ws_x = _dot(ws.astype(BF16), e_mat)

    xdt = xs * dt_x
    xdt_b = xdt.astype(BF16)
    xw_p = pad_rows((xdt * ws_x).astype(BF16))
    lo_half = lax.broadcasted_iota(I32, (LK, LANES), 1) < SSD_HEAD_DIM
    zero_b = jnp.zeros((LK, LANES), BF16)
    dskip = dskip_ref[...]
    neg_inf = jnp.float32(-jnp.inf)

    ys = []
    for g in range(SSD_GROUPS):
        bc_g = bm[:, g * SSD_STATE:(g + 1) * SSD_STATE].astype(BF16)
        cc_g = cm[:, g * SSD_STATE:(g + 1) * SSD_STATE].astype(BF16)
        bc_p = pad_rows(bc_g)
        cb = _dot_nt(cc_g, bc_p)
        h_g = h_scr[g * GROUP_INNER:(g + 1) * GROUP_INNER, :]
        ys_g = _dot_nt(cc_g, h_g.astype(BF16))
        for q in range(HPG // 2):
            ha = g * HPG + 2 * q
            c0 = (g * (HPG // 2) + q) * LANES
            dec = []
            for hh in (ha, ha + 1):
                seg = cumc[:, hh:hh + 1] - cumr[hh:hh + 1, :]
                dec.append(cb * jnp.exp(jnp.where(causal, seg, neg_inf)))
            m_cat = jnp.concatenate(dec, axis=1).astype(BF16)
            x_p = pad_rows(xdt_b[:, c0:c0 + LANES])
            xbd = jnp.concatenate([jnp.where(lo_half, x_p, zero_b), jnp.where(lo_half, zero_b, x_p)], axis=0)
            y_p = _dot(m_cat, xbd)
            y_p = y_p + ys_g[:, q * LANES:(q + 1) * LANES] * ecum_x[:, c0:c0 + LANES]
            y_p = y_p + dskip[:, c0:c0 + LANES] * xs[:, c0:c0 + LANES]
            ys.append(y_p)
        upd = _dot_tn(xw_p[:, g * GROUP_INNER:(g + 1) * GROUP_INNER], bc_p)
        scaled = [h_g[j * SSD_HEAD_DIM:(j + 1) * SSD_HEAD_DIM, :] * elc[g * HPG + j:g * HPG + j + 1, :]
                  for j in range(HPG)]
        h_scr[g * GROUP_INNER:(g + 1) * GROUP_INNER, :] = jnp.concatenate(scaled, axis=0) + upd

    y = jnp.concatenate(ys, axis=1)
    gated = y * _silu(z_ref[...].astype(F32))
    outs = []
    for g in range(SSD_GROUPS):
        gg = gated[:, g * GROUP_INNER:(g + 1) * GROUP_INNER]
        outs.append(gg * lax.rsqrt(jnp.mean(gg * gg, axis=-1, keepdims=True) + RMS_EPS))
    y_ref[...] = (jnp.concatenate(outs, axis=1) * nw_ref[...]).astype(y_ref.dtype)

    @pl.when(c == nc - 1)
    def _():
        sso_ref[...] = h_scr[...].reshape(SSD_HEADS, SSD_HEAD_DIM, SSD_STATE)


def _ssd_core(xbc, z, dt, prm, bsz, t, L, y_dtype, init=None):
    nc = t // L
    m = bsz * t
    has_init = init is not None
    tok = lambda n: pl.BlockSpec((L, n), lambda b, c: (b * nc + c, 0))
    full = lambda a: pl.BlockSpec(a.shape, lambda b, c: (0, 0))
    params = [prm["conv_w"], prm["conv_b"], prm["dt_bias"], prm["a_log"], prm["d_skip"], prm["norm_w"], prm["expand"]]
    in_specs = [tok(CONV_DIM), tok(D_INNER), tok(LANES)] + [full(a) for a in params]
    args = [xbc, z, dt] + params
    if has_init:
        conv0, ssm0, layer = init
        in_specs += [pl.BlockSpec((None, None, SSD_CONV - 1, CONV_DIM), lambda b, c: (layer, b, 0, 0)),
                     pl.BlockSpec((None, None, SSD_HEADS, SSD_HEAD_DIM, SSD_STATE), lambda b, c: (layer, b, 0, 0, 0))]
        args += [conv0, ssm0]
    return pl.pallas_call(
        functools.partial(_ssd_kernel, L=L, has_init=has_init),
        grid=(bsz, nc),
        in_specs=in_specs,
        out_specs=[tok(D_INNER),
                   pl.BlockSpec((None, SSD_CONV - 1, CONV_DIM), lambda b, c: (b, 0, 0)),
                   pl.BlockSpec((None, SSD_HEADS, SSD_HEAD_DIM, SSD_STATE), lambda b, c: (b, 0, 0, 0))],
        out_shape=[jax.ShapeDtypeStruct((m, D_INNER), y_dtype),
                   jax.ShapeDtypeStruct((bsz, SSD_CONV - 1, CONV_DIM), F32),
                   jax.ShapeDtypeStruct((bsz, SSD_HEADS, SSD_HEAD_DIM, SSD_STATE), F32)],
        scratch_shapes=[pltpu.VMEM((L + 8, CONV_DIM), F32), pltpu.VMEM((D_INNER, SSD_STATE), F32)],
        compiler_params=_cparams("parallel", "arbitrary"),
        name="ssd_core_init" if has_init else "ssd_core",
    )(*args)


def _proj_ln_kernel(y_ref, w_ref, xr_ref, g_ref, b_ref, o_ref):
    h = _dot(y_ref[...].astype(BF16), w_ref[...])
    o_ref[...] = _layer_norm(DEEPNORM_ALPHA * xr_ref[...] + h, g_ref[...], b_ref[...])


def _proj_ln(y, w, xres, g, b, tm):
    m, k = y.shape
    full = lambda a: pl.BlockSpec(a.shape, lambda i: (0, 0))
    row = lambda n: pl.BlockSpec((tm, n), lambda i: (i, 0))
    return pl.pallas_call(
        _proj_ln_kernel,
        grid=(m // tm,),
        in_specs=[row(k), full(w), row(D_MODEL), full(g), full(b)],
        out_specs=row(D_MODEL),
        out_shape=jax.ShapeDtypeStruct((m, D_MODEL), F32),
        compiler_params=_cparams("parallel"),
        name="proj_ln",
    )(y, w, xres, g, b)


def _mla_down_kernel(x_ref, w_ref, qn_ref, kvn_ref, cos_ref, sin_ref, cq_ref, ckv_ref, kr_ref, kcat_ref):
    c = _dot(x_ref[...].astype(BF16), w_ref[...])
    cq = _rms_norm(c[:, :Q_LORA], qn_ref[...])
    ckv = _rms_norm(c[:, Q_LORA:Q_LORA + KV_LORA], kvn_ref[...])
    kr = _rope128(c[:, Q_LORA + KV_LORA:], cos_ref[...], sin_ref[...])
    cq_ref[...] = cq.astype(BF16)
    ckv_ref[...] = ckv
    kr_ref[...] = kr[:, :QK_ROPE]
    kcat_ref[...] = jnp.concatenate([ckv, kr], axis=1).astype(kcat_ref.dtype)


def _mla_down(x, w, qn, kvn, cos, sin, kcat_dtype, tm):
    m = x.shape[0]
    nper = cos.shape[0] // tm
    full = lambda a: pl.BlockSpec(a.shape, lambda i: (0, 0))
    row = lambda n: pl.BlockSpec((tm, n), lambda i: (i, 0))
    tab = pl.BlockSpec((tm, LANES), lambda i: (i % nper, 0))
    return pl.pallas_call(
        _mla_down_kernel,
        grid=(m // tm,),
        in_specs=[row(D_MODEL), full(w), full(qn), full(kvn), tab, tab],
        out_specs=[row(Q_LORA), row(KV_LORA), row(QK_ROPE), row(KCAT)],
        out_shape=[jax.ShapeDtypeStruct((m, Q_LORA), BF16),
                   jax.ShapeDtypeStruct((m, KV_LORA), F32),
                   jax.ShapeDtypeStruct((m, QK_ROPE), F32),
                   jax.ShapeDtypeStruct((m, KCAT), kcat_dtype)],
        compiler_params=_cparams("parallel"),
        name="mla_down",
    )(x, w, qn, kvn, cos, sin)


def _mla_q_kernel(cq_ref, wn_ref, wr_ref, wuk_ref, cos_ref, sin_ref, q_ref):
    cq = cq_ref[...]
    qn = _dot(cq, wn_ref[...]).astype(BF16)
    qr = _dot(cq, wr_ref[...])
    cos = cos_ref[...]
    sin = sin_ref[...]
    for h in range(MLA_HEADS):
        ql = _dot(qn[:, h * QK_NOPE:(h + 1) * QK_NOPE], wuk_ref[h]) * ATTN_SCALE
        rp = _rope128(qr[:, h * LANES:(h + 1) * LANES], cos, sin) * ATTN_SCALE
        q_ref[:, h * KCAT:h * KCAT + KV_LORA] = ql.astype(q_ref.dtype)
        q_ref[:, h * KCAT + KV_LORA:(h + 1) * KCAT] = rp.astype(q_ref.dtype)


def _mla_q(cq, wn, wr, wuk, cos, sin, q_dtype, tm):
    m = cq.shape[0]
    nper = cos.shape[0] // tm
    full2 = lambda a: pl.BlockSpec(a.shape, lambda i: (0, 0))
    tab = pl.BlockSpec((tm, LANES), lambda i: (i % nper, 0))
    return pl.pallas_call(
        _mla_q_kernel,
        grid=(m // tm,),
        in_specs=[pl.BlockSpec((tm, Q_LORA), lambda i: (i, 0)), full2(wn), full2(wr),
                  pl.BlockSpec(wuk.shape, lambda i: (0, 0, 0)), tab, tab],
        out_specs=pl.BlockSpec((tm, MLA_HEADS * KCAT), lambda i: (i, 0)),
        out_shape=jax.ShapeDtypeStruct((m, MLA_HEADS * KCAT), q_dtype),
        compiler_params=_cparams("parallel"),
        name="mla_q",
    )(cq, wn, wr, wuk, cos, sin)


def _softmax_step(s, v, m_scr, l_scr, acc_scr):
    m_prev = m_scr[...]
    m_new = jnp.maximum(m_prev, jnp.max(s, axis=-1, keepdims=True))
    corr = jnp.exp(m_prev - m_new)
    p = jnp.exp(s - m_new)
    l_scr[...] = corr * l_scr[...] + jnp.sum(p, axis=-1, keepdims=True)
    acc_scr[...] = corr * acc_scr[...] + _dot(p.astype(BF16), v)
    m_scr[...] = m_new


def _attn_prompt_kernel(qi_ref, kj_ref, q_ref, k_ref, o_ref, qs_scr, m_scr, l_scr, acc_scr, *, tq, tk):
    p = pl.program_id(1)
    i = qi_ref[p]
    j = kj_ref[p]
    rows = MLA_HEADS * tq

    @pl.when(j == 0)
    def _():
        for h in range(MLA_HEADS):
            qs_scr[h * tq:(h + 1) * tq, :] = q_ref[:, h * KCAT:(h + 1) * KCAT]
        m_scr[...] = jnp.full((rows, 1), -jnp.inf, F32)
        l_scr[...] = jnp.zeros((rows, 1), F32)
        acc_scr[...] = jnp.zeros((rows, KV_LORA), F32)

    k = k_ref[...]
    s = _dot_nt(qs_scr[...], k)
    v = k[:, :KV_LORA]
    crosses = (j + 1) * tk - 1 > i * tq

    @pl.when(crosses)
    def _():
        qpos = (lax.broadcasted_iota(I32, (rows, tk), 0) & (tq - 1)) + i * tq
        kpos = lax.broadcasted_iota(I32, (rows, tk), 1) + j * tk
        _softmax_step(jnp.where(kpos <= qpos, s, -jnp.inf), v, m_scr, l_scr, acc_scr)

    @pl.when(jnp.logical_not(crosses))
    def _():
        _softmax_step(s, v, m_scr, l_scr, acc_scr)

    @pl.when(j == ((i + 1) * tq - 1) // tk)
    def _():
        inv = 1.0 / l_scr[...]
        for h in range(MLA_HEADS):
            o_ref[:, h * KV_LORA:(h + 1) * KV_LORA] = (
                acc_scr[h * tq:(h + 1) * tq, :] * inv[h * tq:(h + 1) * tq, :]).astype(o_ref.dtype)


def _attn_prompt(qcat, kcat, bsz, t, tq, tk):
    nq, nk = t // tq, t // tk
    pairs = [(i, j) for i in range(nq) for j in range(((i + 1) * tq - 1) // tk + 1)]
    qi = jnp.asarray([p[0] for p in pairs], I32)
    kj = jnp.asarray([p[1] for p in pairs], I32)
    rows = MLA_HEADS * tq
    grid_spec = pltpu.PrefetchScalarGridSpec(
        num_scalar_prefetch=2,
        grid=(bsz, len(pairs)),
        in_specs=[pl.BlockSpec((tq, MLA_HEADS * KCAT), lambda b, p, qi, kj: (b * nq + qi[p], 0)),
                  pl.BlockSpec((tk, KCAT), lambda b, p, qi, kj: (b * nk + kj[p], 0))],
        out_specs=pl.BlockSpec((tq, MLA_HEADS * KV_LORA), lambda b, p, qi, kj: (b * nq + qi[p], 0)),
        scratch_shapes=[pltpu.VMEM((rows, KCAT), BF16), pltpu.VMEM((rows, 1), F32),
                        pltpu.VMEM((rows, 1), F32), pltpu.VMEM((rows, KV_LORA), F32)],
    )
    return pl.pallas_call(
        functools.partial(_attn_prompt_kernel, tq=tq, tk=tk),
        grid_spec=grid_spec,
        out_shape=jax.ShapeDtypeStruct((bsz * t, MLA_HEADS * KV_LORA), BF16),
        compiler_params=_cparams("parallel", "arbitrary"),
        name="attn_prompt",
    )(qi, kj, qcat, kcat)


def _attn_sample_kernel(pt_ref, q_ref, kn_ref, *rest, npages, t):
    lat_refs = rest[:npages]
    kr_refs = rest[npages:2 * npages]
    o_ref = rest[2 * npages]
    qs_scr, m_scr, l_scr, acc_scr = rest[2 * npages + 1:]
    step = pl.program_id(1)
    rows = MLA_HEADS * t

    @pl.when(step == 0)
    def _():
        for h in range(MLA_HEADS):
            qs_scr[h * t:(h + 1) * t, :] = q_ref[:, h * KCAT:(h + 1) * KCAT]
        m_scr[...] = jnp.full((rows, 1), -jnp.inf, F32)
        l_scr[...] = jnp.zeros((rows, 1), F32)
        acc_scr[...] = jnp.zeros((rows, KV_LORA), F32)

    qs = qs_scr[...].astype(BF16)
    klat = jnp.concatenate([r[...] for r in lat_refs], axis=0).astype(BF16)
    krope_t = jnp.concatenate([r[...] for r in kr_refs], axis=1).astype(BF16)
    s = _dot_nt(qs[:, :KV_LORA], klat) + _dot(qs[:, KV_LORA:KV_LORA + QK_ROPE], krope_t)
    _softmax_step(s, klat, m_scr, l_scr, acc_scr)

    @pl.when(step == pl.num_programs(1) - 1)
    def _():
        kn = jnp.concatenate([kn_ref[...], jnp.zeros((LANES - t, KCAT), F32)], axis=0).astype(BF16)
        sn = _dot_nt(qs, kn)
        qpos = lax.broadcasted_iota(I32, (rows, LANES), 0) & (t - 1)
        kpos = lax.broadcasted_iota(I32, (rows, LANES), 1)
        _softmax_step(jnp.where(kpos <= qpos, sn, -jnp.inf), kn[:, :KV_LORA], m_scr, l_scr, acc_scr)
        inv = 1.0 / l_scr[...]
        for h in range(MLA_HEADS):
            o_ref[:, h * KV_LORA:(h + 1) * KV_LORA] = acc_scr[h * t:(h + 1) * t, :] * inv[h * t:(h + 1) * t, :]


def _attn_sample(qcat, kcat, cache_lat, cache_kr, page_table, layer, bsz, t, npages):
    n_pages = page_table.shape[1]
    nsteps = n_pages // npages

    def page_map(b, s, pt, *, i):
        return (layer, pt[b, s * npages + i], 0, 0)

    lat_specs = [pl.BlockSpec((None, None, PAGE_SIZE, KV_LORA), functools.partial(page_map, i=i)) for i in range(npages)]
    kr_specs = [pl.BlockSpec((None, None, QK_ROPE, PAGE_SIZE), functools.partial(page_map, i=i)) for i in range(npages)]
    rows = MLA_HEADS * t
    grid_spec = pltpu.PrefetchScalarGridSpec(
        num_scalar_prefetch=1,
        grid=(bsz, nsteps),
        in_specs=[pl.BlockSpec((None, t, MLA_HEADS * KCAT), lambda b, s, pt: (b, 0, 0)),
                  pl.BlockSpec((None, t, KCAT), lambda b, s, pt: (b, 0, 0))] + lat_specs + kr_specs,
        out_specs=pl.BlockSpec((None, t, MLA_HEADS * KV_LORA), lambda b, s, pt: (b, 0, 0)),
        scratch_shapes=[pltpu.VMEM((rows, KCAT), F32), pltpu.VMEM((rows, 1), F32),
                        pltpu.VMEM((rows, 1), F32), pltpu.VMEM((rows, KV_LORA), F32)],
    )
    out = pl.pallas_call(
        functools.partial(_attn_sample_kernel, npages=npages, t=t),
        grid_spec=grid_spec,
        out_shape=jax.ShapeDtypeStruct((bsz, t, MLA_HEADS * KV_LORA), F32),
        compiler_params=_cparams("parallel", "arbitrary"),
        name="attn_sample",
    )(page_table, qcat.reshape(bsz, t, -1), kcat.reshape(bsz, t, -1),
      *([cache_lat] * npages), *([jnp.swapaxes(cache_kr, 2, 3)] * npages))
    return out.reshape(bsz * t, -1)


def _mla_out_kernel(o_ref, wuv_ref, wo_ref, xr_ref, g_ref, b_ref, out_ref):
    o = o_ref[...].astype(BF16)
    parts = [_dot(o[:, h * KV_LORA:(h + 1) * KV_LORA], wuv_ref[h]).astype(BF16) for h in range(MLA_HEADS)]
    h = _dot(jnp.concatenate(parts, axis=1), wo_ref[...])
    out_ref[...] = _layer_norm(DEEPNORM_ALPHA * xr_ref[...] + h, g_ref[...], b_ref[...])


def _mla_out(o, wuv, wo, xres, g, b, tm):
    m = o.shape[0]
    full = lambda a: pl.BlockSpec(a.shape, lambda i: (0,) * a.ndim)
    row = lambda n: pl.BlockSpec((tm, n), lambda i: (i, 0))
    return pl.pallas_call(
        _mla_out_kernel,
        grid=(m // tm,),
        in_specs=[row(MLA_HEADS * KV_LORA), full(wuv), full(wo), row(D_MODEL), full(g), full(b)],
        out_specs=row(D_MODEL),
        out_shape=jax.ShapeDtypeStruct((m, D_MODEL), F32),
        compiler_params=_cparams("parallel"),
        name="mla_out",
    )(o, wuv, wo, xres, g, b)


def _route_kernel(x_ref, rh_ref, rl_ref, bias_ref, pa_ref, pb_ref, wa_ref, wb_ref, ng_ref, *, subs):
    tm = subs * MOE_SUB
    xh, xm = _split2(x_ref[...])
    rh = rh_ref[...]
    logits = _dot_nt(rh, xh) + _dot_nt(rh, xm) + _dot_nt(rl_ref[...], xh)
    aff = jax.nn.sigmoid(logits)
    sel = aff + bias_ref[...]
    sub = lax.broadcasted_iota(I32, (EXPERTS_PER_GROUP, tm), 0)
    for g in range(MOE_GROUPS):
        sl = slice(g * EXPERTS_PER_GROUP, (g + 1) * EXPERTS_PER_GROUP)
        sg, ag = sel[sl, :], aff[sl, :]
        m1 = jnp.max(sg, axis=0, keepdims=True)
        i1 = jnp.min(jnp.where(sg == m1, sub, EXPERTS_PER_GROUP), axis=0, keepdims=True)
        sg2 = jnp.where(sub == i1, -jnp.inf, sg)
        m2 = jnp.max(sg2, axis=0, keepdims=True)
        i2 = jnp.min(jnp.where(sg2 == m2, sub, EXPERTS_PER_GROUP), axis=0, keepdims=True)
        a1 = jnp.sum(jnp.where(sub == i1, ag, 0.0), axis=0, keepdims=True)
        a2 = jnp.sum(jnp.where(sub == i2, ag, 0.0), axis=0, keepdims=True)
        score = m1 + m2
        if g == 0:
            best, ea, eb, va, vb = score, i1, i2, a1, a2
        else:
            better = score > best
            best = jnp.where(better, score, best)
            ea = jnp.where(better, i1 + g * EXPERTS_PER_GROUP, ea)
            eb = jnp.where(better, i2 + g * EXPERTS_PER_GROUP, eb)
            va = jnp.where(better, a1, va)
            vb = jnp.where(better, a2, vb)
    den = va + vb
    wa_ref[...] = va / den
    wb_ref[...] = vb / den

    eid = lax.broadcasted_iota(I32, (N_EXPERTS, tm), 0)
    oh_a = jnp.where(eid == ea, 1.0, 0.0)
    oh_b = jnp.where(eid == eb, 1.0, 0.0)
    before = lax.broadcasted_iota(I32, (MOE_SUB, MOE_SUB), 0) < lax.broadcasted_iota(I32, (MOE_SUB, MOE_SUB), 1)
    upper = jnp.where(before, 1.0, 0.0).astype(BF16)
    lower = jnp.where(lax.broadcasted_iota(I32, (N_EXPERTS, N_EXPERTS), 1)
                      < lax.broadcasted_iota(I32, (N_EXPERTS, N_EXPERTS), 0), 1.0, 0.0).astype(BF16)
    lane = lax.broadcasted_iota(I32, (N_EXPERTS, LANES), 1)
    ng_all = jnp.zeros((N_EXPERTS, LANES), F32)
    pos_a, pos_b = [], []
    for s in range(subs):
        a = oh_a[:, s * MOE_SUB:(s + 1) * MOE_SUB]
        b = oh_b[:, s * MOE_SUB:(s + 1) * MOE_SUB]
        cnt_a = jnp.sum(a, axis=1, keepdims=True)
        cnt = cnt_a + jnp.sum(b, axis=1, keepdims=True)
        ng = jnp.floor((cnt + (MOE_GRAN - 1)) * (1.0 / MOE_GRAN))
        rows = jnp.broadcast_to(ng * MOE_GRAN, (N_EXPERTS, LANES)).astype(BF16)
        start = _dot(lower, rows)[:, :1]
        pos_a.append(jnp.sum(a * (start + _dot(a.astype(BF16), upper)), axis=0, keepdims=True))
        pos_b.append(jnp.sum(b * (start + cnt_a + _dot(b.astype(BF16), upper)), axis=0, keepdims=True))
        ng_all = jnp.where(lane == s, ng, ng_all)
    pa_ref[...] = jnp.concatenate(pos_a, axis=1).astype(I32)
    pb_ref[...] = jnp.concatenate(pos_b, axis=1).astype(I32)
    ng_ref[...] = ng_all


def _route(x, rh, rl, bias, subs):
    m = x.shape[0]
    tm = subs * MOE_SUB
    full = lambda a: pl.BlockSpec(a.shape, lambda i: (0, 0))
    lane_row = pl.BlockSpec((1, tm), lambda i: (0, i))
    row_i = jax.ShapeDtypeStruct((1, m), I32)
    row_f = jax.ShapeDtypeStruct((1, m), F32)
    return pl.pallas_call(
        functools.partial(_route_kernel, subs=subs),
        grid=(m // tm,),
        in_specs=[pl.BlockSpec((tm, D_MODEL), lambda i: (i, 0)), full(rh), full(rl), full(bias)],
        out_specs=[lane_row, lane_row, lane_row, lane_row,
                   pl.BlockSpec((None, N_EXPERTS, LANES), lambda i: (i, 0, 0))],
        out_shape=[row_i, row_i, row_f, row_f, jax.ShapeDtypeStruct((m // tm, N_EXPERTS, LANES), F32)],
        compiler_params=_cparams("parallel"),
        name="moe_route",
    )(x, rh, rl, bias)


def _moe_kernel(ng_ref, gs_ref, nit_ref, x_ref, pa_ref, pb_ref, wa_ref, wb_ref, wg_ref, wu_ref, wd_ref,
                g_ref, b_ref, o_ref, buf, *, ns, subs, eps):
    c = pl.program_id(0)
    st = pl.program_id(1)
    n_sort = ns // subs
    n_exp = N_EXPERTS // eps
    row_id = lax.broadcasted_iota(I32, (MOE_ROWS, MOE_SUB), 0)

    @pl.when(st < n_sort)
    def _():
        for k in range(subs):
            sl = slice(k * MOE_SUB, (k + 1) * MOE_SUB)
            hit = jnp.logical_or(row_id == pa_ref[:, sl], row_id == pb_ref[:, sl])
            perm = jnp.where(hit, 1.0, 0.0).astype(BF16)
            s = st * subs + k
            buf[s, 0:MOE_ROWS, :] = _dot(perm, x_ref[sl, :].astype(BF16)).astype(BF16)
            buf[s, MOE_ROWS:MOE_ROWS + MOE_GRAN, :] = jnp.zeros((MOE_GRAN, D_MODEL), BF16)

    @pl.when(jnp.logical_and(st >= n_sort, st < n_sort + n_exp))
    def _():
        for j in range(eps):
            e = (st - n_sort) * eps + j

            def body(w, carry, j=j, e=e):
                starts = []
                for s in range(ns):
                    idx = (c * ns + s) * N_EXPERTS + e
                    row0 = jnp.where(w < ng_ref[idx], gs_ref[idx] + w * MOE_GRAN, MOE_ROWS)
                    starts.append(pl.multiple_of(row0, MOE_GRAN))
                xw = jnp.concatenate([buf[s, pl.ds(starts[s], MOE_GRAN), :] for s in range(ns)], axis=0)
                h = _silu(_dot(xw, wg_ref[j])) * _dot(xw, wu_ref[j])
                y = _dot(h.astype(BF16), wd_ref[j]).astype(BF16)
                for s in range(ns):
                    buf[s, pl.ds(starts[s], MOE_GRAN), :] = y[s * MOE_GRAN:(s + 1) * MOE_GRAN, :]
                return carry

            lax.fori_loop(0, nit_ref[c * N_EXPERTS + e], body, 0)

    @pl.when(st >= n_sort + n_exp)
    def _():
        for k in range(subs):
            sl = slice(k * MOE_SUB, (k + 1) * MOE_SUB)
            gate = (jnp.where(row_id == pa_ref[:, sl], wa_ref[:, sl], 0.0)
                    + jnp.where(row_id == pb_ref[:, sl], wb_ref[:, sl], 0.0)).astype(BF16)
            s = (st - n_sort - n_exp) * subs + k
            y = _dot_tn(gate, buf[s, 0:MOE_ROWS, :])
            o_ref[sl, :] = _layer_norm(DEEPNORM_ALPHA * x_ref[sl, :] + y, g_ref[...], b_ref[...])


def _hier_moe_ln(x, prm, g, b, ns):
    m = x.shape[0]
    subs, eps = MOE_SUBS_PER_STEP, MOE_EXPERTS_PER_STEP
    tm = subs * MOE_SUB
    chunk = ns * MOE_SUB
    n_chunks = m // chunk
    n_sort = ns // subs
    n_exp = N_EXPERTS // eps
    pa, pb, wa, wb, ng = _route(x, prm["router_hi"], prm["router_lo"], prm["router_bias"], subs)
    ng = jnp.transpose(ng[:, :, :subs].astype(I32), (0, 2, 1)).reshape(n_chunks, ns, N_EXPERTS)
    gstart = (jnp.cumsum(ng, axis=-1) - ng) * MOE_GRAN
    nit = jnp.max(ng, axis=1)

    def tok_idx(c, st, *_):
        phase = jnp.where(st < n_sort, st, jnp.where(st >= n_sort + n_exp, st - n_sort - n_exp, n_sort - 1))
        return c * n_sort + phase

    def exp_idx(c, st, *_):
        return (jnp.clip(st - n_sort, 0, n_exp - 1), 0, 0)

    lane_row = pl.BlockSpec((1, tm), lambda c, st, *_: (0, tok_idx(c, st)))
    full = lambda a: pl.BlockSpec(a.shape, lambda c, st, *_: (0, 0))
    grid_spec = pltpu.PrefetchScalarGridSpec(
        num_scalar_prefetch=3,
        grid=(n_chunks, 2 * n_sort + n_exp),
        in_specs=[pl.BlockSpec((tm, D_MODEL), lambda c, st, *_: (tok_idx(c, st), 0)),
                  lane_row, lane_row, lane_row, lane_row,
                  pl.BlockSpec((eps, D_MODEL, EXPERT_FF), exp_idx),
                  pl.BlockSpec((eps, D_MODEL, EXPERT_FF), exp_idx),
                  pl.BlockSpec((eps, EXPERT_FF, D_MODEL), exp_idx),
                  full(g), full(b)],
        out_specs=pl.BlockSpec((tm, D_MODEL),
                               lambda c, st, *_: (c * n_sort + jnp.clip(st - n_sort - n_exp, 0, n_sort - 1), 0)),
        scratch_shapes=[pltpu.VMEM((ns, MOE_ROWS + MOE_GRAN, D_MODEL), BF16)],
    )
    return pl.pallas_call(
        functools.partial(_moe_kernel, ns=ns, subs=subs, eps=eps),
        grid_spec=grid_spec,
        out_shape=jax.ShapeDtypeStruct((m, D_MODEL), F32),
        compiler_params=_cparams("arbitrary", "arbitrary"),
        name="moe_experts",
    )(ng.reshape(-1), gstart.reshape(-1), nit.reshape(-1), x, pa, pb, wa, wb,
      prm["w_gate"], prm["w_up"], prm["w_down"], g, b)


def _pad_cols(a, n):
    return jnp.pad(a, ((0, 0), (0, n - a.shape[1])))


def _ssd_params(w_in, conv_w, conv_b, dt_bias, a_log, d_skip, norm_w, w_out):
    wdt = _pad_cols(w_in[:, D_INNER + CONV_DIM:], LANES)
    wdh = wdt.astype(BF16)
    head_of_col = jnp.arange(D_INNER, dtype=I32) // SSD_HEAD_DIM
    expand = (jnp.arange(LANES, dtype=I32)[:, None] == head_of_col[None, :]).astype(BF16)
    return {
        "wz": w_in[:, :D_INNER].astype(BF16),
        "wx": w_in[:, D_INNER:D_INNER + CONV_DIM].astype(BF16),
        "wdh": wdh,
        "wdl": (wdt - wdh.astype(F32)).astype(BF16),
        "conv_w": conv_w,
        "conv_b": conv_b[None, :],
        "dt_bias": _pad_cols(dt_bias[None, :], LANES),
        "a_log": _pad_cols(a_log[None, :], LANES),
        "d_skip": jnp.repeat(d_skip, SSD_HEAD_DIM)[None, :],
        "norm_w": norm_w[None, :],
        "expand": expand,
        "w_out": w_out.astype(BF16),
    }


def _mla_params(w_dqkv, q_norm, kv_norm, w_uq, w_uk, w_uv, w_o):
    wr = jnp.pad(w_uq[:, :, QK_NOPE:], ((0, 0), (0, 0), (0, LANES - QK_ROPE)))
    return {
        "w_down": _pad_cols(w_dqkv, DCOLS).astype(BF16),
        "q_norm": q_norm[None, :],
        "kv_norm": kv_norm[None, :],
        "w_qn": w_uq[:, :, :QK_NOPE].reshape(Q_LORA, MLA_HEADS * QK_NOPE).astype(BF16),
        "w_qr": wr.reshape(Q_LORA, MLA_HEADS * LANES).astype(BF16),
        "w_uk_t": jnp.transpose(w_uk, (1, 2, 0)).astype(BF16),
        "w_uv": jnp.transpose(w_uv, (1, 0, 2)).astype(BF16),
        "w_o": w_o.astype(BF16),
    }


def _moe_params(router, router_bias, w_gate, w_up, w_down):
    rt = router.T
    rh = rt.astype(BF16)
    return {
        "router_hi": rh,
        "router_lo": (rt - rh.astype(F32)).astype(BF16),
        "router_bias": router_bias[:, None],
        "w_gate": w_gate.astype(BF16),
        "w_up": w_up.astype(BF16),
        "w_down": w_down.astype(BF16),
    }


def _rope_tables(pos):
    half = QK_ROPE // 2
    inv_freq = ROPE_THETA ** (-jnp.arange(half, dtype=F32) / half)
    ang = pos.astype(F32)[:, None] * inv_freq[None, :]
    cos, sin = jnp.cos(ang), jnp.sin(ang)
    zeros = jnp.zeros((pos.shape[0], LANES - QK_ROPE), F32)
    return jnp.concatenate([cos, cos, zeros], axis=1), jnp.concatenate([-sin, sin, zeros], axis=1)


def _tile(m, want):
    return min(m, want)


def kernel(x_prompt, x_sample, cache_mla_latent, cache_mla_krope, state_ssd, state_conv, page_table,
           ln_mix_g, ln_mix_b, ln_ffn_g, ln_ffn_b,
           ssd_w_in, ssd_conv_w, ssd_conv_b, ssd_dt_bias, ssd_a_log, ssd_d_skip, ssd_norm_w, ssd_w_out,
           mla_w_dqkv, mla_q_norm, mla_kv_norm, mla_w_uq, mla_w_uk, mla_w_uv, mla_w_o,
           moe_router, moe_router_bias, moe_w_gate, moe_w_up, moe_w_down):
    bp, tp, _ = x_prompt.shape
    bs, ts, _ = x_sample.shape
    past_len = page_table.shape[1] * PAGE_SIZE
    xp = x_prompt.reshape(bp * tp, D_MODEL)
    xs = x_sample.reshape(bs * ts, D_MODEL)
    lp = min(SSD_CHUNK, tp)

    cos_p, sin_p = _rope_tables(jnp.arange(tp))
    cos_s, sin_s = _rope_tables(past_len + jnp.arange(ts))
    cos_s = jnp.tile(cos_s, (bs, 1))
    sin_s = jnp.tile(sin_s, (bs, 1))

    p_lat, p_kr, p_ssm, p_conv = [], [], [], []
    s_lat, s_kr, s_ssm, s_conv = [], [], [], []
    for i in range(DEPTH):
        j = i // 2
        g_mix, b_mix = ln_mix_g[i][None, :], ln_mix_b[i][None, :]
        if i % 2 == 0:
            prm = _ssd_params(ssd_w_in[j], ssd_conv_w[j], ssd_conv_b[j], ssd_dt_bias[j], ssd_a_log[j],
                              ssd_d_skip[j], ssd_norm_w[j], ssd_w_out[j])
            z, xbc, dt = _ssd_inproj(xp, prm["wz"], prm["wx"], prm["wdh"], prm["wdl"], BF16, _tile(xp.shape[0], 256))
            y, cp, sp = _ssd_core(xbc, z, dt, prm, bp, tp, lp, BF16)
            xp = _proj_ln(y, prm["w_out"], xp, g_mix, b_mix, _tile(xp.shape[0], 512))
            z, xbc, dt = _ssd_inproj(xs, prm["wz"], prm["wx"], prm["wdh"], prm["wdl"], F32, _tile(xs.shape[0], 256))
            y, cs, ss = _ssd_core(xbc, z, dt, prm, bs, ts, ts, F32, init=(state_conv, state_ssd, j))
            xs = _proj_ln(y, prm["w_out"], xs, g_mix, b_mix, _tile(xs.shape[0], 512))
            p_conv.append(cp)
            p_ssm.append(sp)
            s_conv.append(cs)
            s_ssm.append(ss)
        else:
            prm = _mla_params(mla_w_dqkv[j], mla_q_norm[j], mla_kv_norm[j], mla_w_uq[j], mla_w_uk[j],
                              mla_w_uv[j], mla_w_o[j])
            tm = _tile(tp, 512)
            cq, lat, kr, kcat = _mla_down(xp, prm["w_down"], prm["q_norm"], prm["kv_norm"], cos_p, sin_p, BF16, tm)
            qcat = _mla_q(cq, prm["w_qn"], prm["w_qr"], prm["w_uk_t"], cos_p, sin_p, BF16, tm)
            o = _attn_prompt(qcat, kcat, bp, tp, tm, tm)
            xp = _mla_out(o, prm["w_uv"], prm["w_o"], xp, g_mix, b_mix, tm)
            p_lat.append(lat.reshape(bp, tp, KV_LORA))
            p_kr.append(kr.reshape(bp, tp, QK_ROPE))
            tm = _tile(xs.shape[0], 512)
            cq, lat, kr, kcat = _mla_down(xs, prm["w_down"], prm["q_norm"], prm["kv_norm"], cos_s, sin_s, F32, tm)
            qcat = _mla_q(cq, prm["w_qn"], prm["w_qr"], prm["w_uk_t"], cos_s, sin_s, F32, tm)
            o = _attn_sample(qcat, kcat, cache_mla_latent, cache_mla_krope, page_table, j, bs, ts,
                             min(16, page_table.shape[1]))
            xs = _mla_out(o, prm["w_uv"], prm["w_o"], xs, g_mix, b_mix, tm)
            s_lat.append(lat.reshape(bs, ts, KV_LORA))
            s_kr.append(kr.reshape(bs, ts, QK_ROPE))
        mprm = _moe_params(moe_router[i], moe_router_bias[i], moe_w_gate[i], moe_w_up[i], moe_w_down[i])
        g_ffn, b_ffn = ln_ffn_g[i][None, :], ln_ffn_b[i][None, :]
        xp = _hier_moe_ln(xp, mprm, g_ffn, b_ffn, min(8, xp.shape[0] // MOE_SUB))
        xs = _hier_moe_ln(xs, mprm, g_ffn, b_ffn, min(8, xs.shape[0] // MOE_SUB))

    return (xp.reshape(bp, tp, D_MODEL), xs.reshape(bs, ts, D_MODEL),
            jnp.stack(p_lat), jnp.stack(p_kr), jnp.stack(p_ssm), jnp.stack(p_conv),
            jnp.stack(s_lat), jnp.stack(s_kr), jnp.stack(s_ssm), jnp.stack(s_conv))
```

```python
import functools

import jax
import jax.numpy as jnp
from jax import lax
from jax.experimental import pallas as pl
from jax.experimental.pallas import tpu as pltpu

F32 = jnp.float32
BF16 = jnp.bfloat16
I32 = jnp.int32

D_MODEL = 1024
DEPTH = 4
PAGE_SIZE = 128
D_INNER = 2048
SSD_HEAD_DIM = 64
SSD_HEADS = 32
SSD_GROUPS = 4
SSD_STATE = 128
SSD_CONV = 4
SSD_CHUNK = 128
CONV_DIM = D_INNER + 2 * SSD_GROUPS * SSD_STATE
GROUP_INNER = D_INNER // SSD_GROUPS
MLA_HEADS = 8
Q_LORA = 384
KV_LORA = 256
QK_NOPE = 128
QK_ROPE = 64
V_HEAD = 128
ROPE_THETA = 10000.0
MOE_GROUPS = 4
EXPERTS_PER_GROUP = 8
N_EXPERTS = 32
EXPERT_FF = 256
DEEPNORM_ALPHA = (2.0 * DEPTH) ** 0.25
LN_EPS = 1e-5
RMS_EPS = 1e-6
ATTN_SCALE = (QK_NOPE + QK_ROPE) ** -0.5

LANES = 128
KCAT = KV_LORA + LANES
DCOLS = Q_LORA + KV_LORA + LANES
MOE_SUB = 256
MOE_GRAN = 16
MOE_ROWS = 2 * MOE_SUB + N_EXPERTS * MOE_GRAN
MOE_SUBS_PER_STEP = 2
MOE_EXPERTS_PER_STEP = 4
VMEM_LIMIT = 56 * 1024 * 1024


def _cparams(*sem):
    return pltpu.CompilerParams(dimension_semantics=sem, vmem_limit_bytes=VMEM_LIMIT)


def _dot(a, b):
    return jnp.dot(a, b, preferred_element_type=F32)


def _dot_nt(a, b):
    return lax.dot_general(a, b, (((1,), (1,)), ((), ())), preferred_element_type=F32)


def _dot_tn(a, b):
    return lax.dot_general(a, b, (((0,), (0,)), ((), ())), preferred_element_type=F32)


def _split2(x):
    hi = x.astype(BF16)
    return hi, (x - hi.astype(F32)).astype(BF16)


def _split3(x):
    hi = x.astype(BF16)
    r = x - hi.astype(F32)
    mid = r.astype(BF16)
    return hi, mid, (r - mid.astype(F32)).astype(BF16)


def _silu(x):
    return x * jax.nn.sigmoid(x)


def _softplus(x):
    return jnp.maximum(x, 0.0) + jnp.log1p(jnp.exp(-jnp.abs(x)))


def _layer_norm(v, g, b):
    mu = jnp.mean(v, axis=-1, keepdims=True)
    d = v - mu
    var = jnp.mean(d * d, axis=-1, keepdims=True)
    return d * lax.rsqrt(var + LN_EPS) * g + b


def _rms_norm(v, g):
    return v * lax.rsqrt(jnp.mean(v * v, axis=-1, keepdims=True) + RMS_EPS) * g


def _rope128(x, cos, sin):
    lane = lax.broadcasted_iota(I32, x.shape, 1)
    partner = jnp.where(lane < QK_ROPE // 2, pltpu.roll(x, LANES - QK_ROPE // 2, 1), pltpu.roll(x, QK_ROPE // 2, 1))
    return x * cos + partner * sin


def _ssd_inproj_kernel(x_ref, wz_ref, wx_ref, wdh_ref, wdl_ref, z_ref, xbc_ref, dt_ref):
    x = x_ref[...]
    xh, xm = _split2(x)
    z_ref[...] = _dot(xh, wz_ref[...]).astype(z_ref.dtype)
    xbc_ref[...] = _dot(xh, wx_ref[...])
    wdh = wdh_ref[...]
    dt_ref[...] = _dot(xh, wdh) + _dot(xm, wdh) + _dot(xh, wdl_ref[...])


def _ssd_inproj(x, wz, wx, wdh, wdl, z_dtype, tm):
    m = x.shape[0]
    full = lambda a: pl.BlockSpec(a.shape, lambda i: (0, 0))
    row = lambda n: pl.BlockSpec((tm, n), lambda i: (i, 0))
    return pl.pallas_call(
        _ssd_inproj_kernel,
        grid=(m // tm,),
        in_specs=[row(D_MODEL), full(wz), full(wx), full(wdh), full(wdl)],
        out_specs=[row(D_INNER), row(CONV_DIM), row(LANES)],
        out_shape=[jax.ShapeDtypeStruct((m, D_INNER), z_dtype),
                   jax.ShapeDtypeStruct((m, CONV_DIM), F32),
                   jax.ShapeDtypeStruct((m, LANES), F32)],
        compiler_params=_cparams("parallel"),
        name="ssd_inproj",
    )(x, wz, wx, wdh, wdl)


def _ssd_kernel(*refs, L, has_init):
    if has_init:
        (xbc_ref, z_ref, dt_ref, cw_ref, cb_ref, dtb_ref, alog_ref, dskip_ref, nw_ref, e_ref,
         conv0_ref, ssm0_ref, y_ref, convo_ref, sso_ref, xp_scr, h_scr) = refs
    else:
        (xbc_ref, z_ref, dt_ref, cw_ref, cb_ref, dtb_ref, alog_ref, dskip_ref, nw_ref, e_ref,
         y_ref, convo_ref, sso_ref, xp_scr, h_scr) = refs
    c = pl.program_id(1)
    nc = pl.num_programs(1)
    LK = max(L, LANES)
    K1 = SSD_CONV - 1
    HPG = SSD_HEADS // SSD_GROUPS

    @pl.when(c == 0)
    def _():
        if has_init:
            xp_scr[8 - K1:8, :] = conv0_ref[...]
            h_scr[...] = ssm0_ref[...].reshape(D_INNER, SSD_STATE)
        else:
            xp_scr[0:8, :] = jnp.zeros((8, CONV_DIM), F32)
            h_scr[...] = jnp.zeros((D_INNER, SSD_STATE), F32)

    def pad_rows(a):
        if L == LK:
            return a
        return jnp.concatenate([a, jnp.zeros((LK - L, a.shape[1]), a.dtype)], axis=0)

    xraw = xbc_ref[...]
    xp_scr[8:8 + L, :] = xraw
    acc = cb_ref[...] + xp_scr[5:5 + L, :] * cw_ref[0:1, :]
    acc = acc + xp_scr[6:6 + L, :] * cw_ref[1:2, :]
    acc = acc + xp_scr[7:7 + L, :] * cw_ref[2:3, :]
    acc = acc + xraw * cw_ref[3:4, :]
    conv = _silu(acc)
    tail = xp_scr[L + 8 - K1:L + 8, :]

    @pl.when(c == nc - 1)
    def _():
        convo_ref[...] = tail

    xp_scr[8 - K1:8, :] = tail

    xs = conv[:, :D_INNER]
    bm = conv[:, D_INNER:D_INNER + SSD_GROUPS * SSD_STATE]
    cm = conv[:, D_INNER + SSD_GROUPS * SSD_STATE:]

    dtv = _softplus(dt_ref[...] + dtb_ref[...])
    d_a = dtv * (-jnp.exp(alog_ref[...]))

    row = lax.broadcasted_iota(I32, (L, LK), 0)
    col = lax.broadcasted_iota(I32, (L, LK), 1)
    causal = col <= row
    tril = jnp.where(causal, 1.0, 0.0).astype(BF16)
    cumc = sum(_dot(tril, p) for p in _split3(pad_rows(d_a)))
    ir = lax.broadcasted_iota(I32, (LANES, LANES), 0)
    ic = lax.broadcasted_iota(I32, (LANES, LANES), 1)
    ident = jnp.where(ir == ic, 1.0, 0.0).astype(BF16)
    cumr = sum(_dot_nt(ident, p) for p in _split3(pad_rows(cumc)))

    ecum = jnp.exp(cumc)
    last = cumc[L - 1:L, :]
    ws = jnp.exp(last - cumc)
    elc = jnp.exp(cumr[:, L - 1:L])
    e_mat = e_ref[...]
    dt_x = _dot(dtv.astype(BF16), e_mat)
    ecum_x = _dot(ecum.astype(BF16), e_mat)
    ws_x = _dot(ws.astype(BF16), e_mat)

    xdt = xs * dt_x
    xdt_b = xdt.astype(BF16)
    xw_p = pad_rows((xdt * ws_x).astype(BF16))
    lo_half = lax.broadcasted_iota(I32, (LK, LANES), 1) < SSD_HEAD_DIM
    zero_b = jnp.zeros((LK, LANES), BF16)
    dskip = dskip_ref[...]
    neg_inf = jnp.float32(-jnp.inf)

    ys = []
    for g in range(SSD_GROUPS):
        bc_g = bm[:, g * SSD_STATE:(g + 1) * SSD_STATE].astype(BF16)
        cc_g = cm[:, g * SSD_STATE:(g + 1) * SSD_STATE].astype(BF16)
        bc_p = pad_rows(bc_g)
        cb = _dot_nt(cc_g, bc_p)
        h_g = h_scr[g * GROUP_INNER:(g + 1) * GROUP_INNER, :]
        ys_g = _dot_nt(cc_g, h_g.astype(BF16))
        for q in range(HPG // 2):
            ha = g * HPG + 2 * q
            c0 = (g * (HPG // 2) + q) * LANES
            dec = []
            for hh in (ha, ha + 1):
                seg = cumc[:, hh:hh + 1] - cumr[hh:hh + 1, :]
                dec.append(cb * jnp.exp(jnp.where(causal, seg, neg_inf)))
            m_cat = jnp.concatenate(dec, axis=1).astype(BF16)
            x_p = pad_rows(xdt_b[:, c0:c0 + LANES])
            xbd = jnp.concatenate([jnp.where(lo_half, x_p, zero_b), jnp.where(lo_half, zero_b, x_p)], axis=0)
            y_p = _dot(m_cat, xbd)
            y_p = y_p + ys_g[:, q * LANES:(q + 1) * LANES] * ecum_x[:, c0:c0 + LANES]
            y_p = y_p + dskip[:, c0:c0 + LANES] * xs[:, c0:c0 + LANES]
            ys.append(y_p)
        upd = _dot_tn(xw_p[:, g * GROUP_INNER:(g + 1) * GROUP_INNER], bc_p)
        scaled = [h_g[j * SSD_HEAD_DIM:(j + 1) * SSD_HEAD_DIM, :] * elc[g * HPG + j:g * HPG + j + 1, :]
                  for j in range(HPG)]
        h_scr[g * GROUP_INNER:(g + 1) * GROUP_INNER, :] = jnp.concatenate(scaled, axis=0) + upd

    y = jnp.concatenate(ys, axis=1)
    gated = y * _silu(z_ref[...].astype(F32))
    outs = []
    for g in range(SSD_GROUPS):
        gg = gated[:, g * GROUP_INNER:(g + 1) * GROUP_INNER]
        outs.append(gg * lax.rsqrt(jnp.mean(gg * gg, axis=-1, keepdims=True) + RMS_EPS))
    y_ref[...] = (jnp.concatenate(outs, axis=1) * nw_ref[...]).astype(y_ref.dtype)

    @pl.when(c == nc - 1)
    def _():
        sso_ref[...] = h_scr[...].reshape(SSD_HEADS, SSD_HEAD_DIM, SSD_STATE)


def _ssd_core(xbc, z, dt, prm, bsz, t, L, y_dtype, init=None):
    nc = t // L
    m = bsz * t
    has_init = init is not None
    tok = lambda n: pl.BlockSpec((L, n), lambda b, c: (b * nc + c, 0))
    full = lambda a: pl.BlockSpec(a.shape, lambda b, c: (0, 0))
    params = [prm["conv_w"], prm["conv_b"], prm["dt_bias"], prm["a_log"], prm["d_skip"], prm["norm_w"], prm["expand"]]
    in_specs = [tok(CONV_DIM), tok(D_INNER), tok(LANES)] + [full(a) for a in params]
    args = [xbc, z, dt] + params
    if has_init:
        conv0, ssm0, layer = init
        in_specs += [pl.BlockSpec((None, None, SSD_CONV - 1, CONV_DIM), lambda b, c: (layer, b, 0, 0)),
                     pl.BlockSpec((None, None, SSD_HEADS, SSD_HEAD_DIM, SSD_STATE), lambda b, c: (layer, b, 0, 0, 0))]
        args += [conv0, ssm0]
    return pl.pallas_call(
        functools.partial(_ssd_kernel, L=L, has_init=has_init),
        grid=(bsz, nc),
        in_specs=in_specs,
        out_specs=[tok(D_INNER),
                   pl.BlockSpec((None, SSD_CONV - 1, CONV_DIM), lambda b, c: (b, 0, 0)),
                   pl.BlockSpec((None, SSD_HEADS, SSD_HEAD_DIM, SSD_STATE), lambda b, c: (b, 0, 0, 0))],
        out_shape=[jax.ShapeDtypeStruct((m, D_INNER), y_dtype),
                   jax.ShapeDtypeStruct((bsz, SSD_CONV - 1, CONV_DIM), F32),
                   jax.ShapeDtypeStruct((bsz, SSD_HEADS, SSD_HEAD_DIM, SSD_STATE), F32)],
        scratch_shapes=[pltpu.VMEM((L + 8, CONV_DIM), F32), pltpu.VMEM((D_INNER, SSD_STATE), F32)],
        compiler_params=_cparams("parallel", "arbitrary"),
        name="ssd_core_init" if has_init else "ssd_core",
    )(*args)


def _proj_ln_kernel(y_ref, w_ref, xr_ref, g_ref, b_ref, o_ref):
    h = _dot(y_ref[...].astype(BF16), w_ref[...])
    o_ref[...] = _layer_norm(DEEPNORM_ALPHA * xr_ref[...] + h, g_ref[...], b_ref[...])


def _proj_ln(y, w, xres, g, b, tm):
    m, k = y.shape
    full = lambda a: pl.BlockSpec(a.shape, lambda i: (0, 0))
    row = lambda n: pl.BlockSpec((tm, n), lambda i: (i, 0))
    return pl.pallas_call(
        _proj_ln_kernel,
        grid=(m // tm,),
        in_specs=[row(k), full(w), row(D_MODEL), full(g), full(b)],
        out_specs=row(D_MODEL),
        out_shape=jax.ShapeDtypeStruct((m, D_MODEL), F32),
        compiler_params=_cparams("parallel"),
        name="proj_ln",
    )(y, w, xres, g, b)


def _mla_down_kernel(x_ref, w_ref, qn_ref, kvn_ref, cos_ref, sin_ref, cq_ref, ckv_ref, kr_ref, kcat_ref):
    c = _dot(x_ref[...].astype(BF16), w_ref[...])
    cq = _rms_norm(c[:, :Q_LORA], qn_ref[...])
    ckv = _rms_norm(c[:, Q_LORA:Q_LORA + KV_LORA], kvn_ref[...])
    kr = _rope128(c[:, Q_LORA + KV_LORA:], cos_ref[...], sin_ref[...])
    cq_ref[...] = cq.astype(BF16)
    ckv_ref[...] = ckv
    kr_ref[...] = kr[:, :QK_ROPE]
    kcat_ref[...] = jnp.concatenate([ckv, kr], axis=1).astype(kcat_ref.dtype)


def _mla_down(x, w, qn, kvn, cos, sin, kcat_dtype, tm):
    m = x.shape[0]
    nper = cos.shape[0] // tm
    full = lambda a: pl.BlockSpec(a.shape, lambda i: (0, 0))
    row = lambda n: pl.BlockSpec((tm, n), lambda i: (i, 0))
    tab = pl.BlockSpec((tm, LANES), lambda i: (i % nper, 0))
    return pl.pallas_call(
        _mla_down_kernel,
        grid=(m // tm,),
        in_specs=[row(D_MODEL), full(w), full(qn), full(kvn), tab, tab],
        out_specs=[row(Q_LORA), row(KV_LORA), row(QK_ROPE), row(KCAT)],
        out_shape=[jax.ShapeDtypeStruct((m, Q_LORA), BF16),
                   jax.ShapeDtypeStruct((m, KV_LORA), F32),
                   jax.ShapeDtypeStruct((m, QK_ROPE), F32),
                   jax.ShapeDtypeStruct((m, KCAT), kcat_dtype)],
        compiler_params=_cparams("parallel"),
        name="mla_down",
    )(x, w, qn, kvn, cos, sin)


def _mla_q_kernel(cq_ref, wn_ref, wr_ref, wuk_ref, cos_ref, sin_ref, q_ref):
    cq = cq_ref[...]
    qn = _dot(cq, wn_ref[...]).astype(BF16)
    qr = _dot(cq, wr_ref[...])
    cos = cos_ref[...]
    sin = sin_ref[...]
    for h in range(MLA_HEADS):
        ql = _dot(qn[:, h * QK_NOPE:(h + 1) * QK_NOPE], wuk_ref[h]) * ATTN_SCALE
        rp = _rope128(qr[:, h * LANES:(h + 1) * LANES], cos, sin) * ATTN_SCALE
        q_ref[:, h * KCAT:h * KCAT + KV_LORA] = ql.astype(q_ref.dtype)
        q_ref[:, h * KCAT + KV_LORA:(h + 1) * KCAT] = rp.astype(q_ref.dtype)


def _mla_q(cq, wn, wr, wuk, cos, sin, q_dtype, tm):
    m = cq.shape[0]
    nper = cos.shape[0] // tm
    full2 = lambda a: pl.BlockSpec(a.shape, lambda i: (0, 0))
    tab = pl.BlockSpec((tm, LANES), lambda i: (i % nper, 0))
    return pl.pallas_call(
        _mla_q_kernel,
        grid=(m // tm,),
        in_specs=[pl.BlockSpec((tm, Q_LORA), lambda i: (i, 0)), full2(wn), full2(wr),
                  pl.BlockSpec(wuk.shape, lambda i: (0, 0, 0)), tab, tab],
        out_specs=pl.BlockSpec((tm, MLA_HEADS * KCAT), lambda i: (i, 0)),
        out_shape=jax.ShapeDtypeStruct((m, MLA_HEADS * KCAT), q_dtype),
        compiler_params=_cparams("parallel"),
        name="mla_q",
    )(cq, wn, wr, wuk, cos, sin)


def _lane_tile(a, n):
    return a if n == 1 else jnp.concatenate([a] * n, axis=1)


def _softmax_step(s, v, m_scr, l_scr, acc_scr):
    m_prev = m_scr[...]
    m_new = jnp.maximum(m_prev, jnp.max(s, axis=-1, keepdims=True))
    corr = jnp.exp(m_prev - m_new)
    p = jnp.exp(s - _lane_tile(m_new, s.shape[1] // LANES))
    l_scr[...] = corr * l_scr[...] + jnp.sum(p, axis=-1, keepdims=True)
    acc_scr[...] = _lane_tile(corr, KV_LORA // LANES) * acc_scr[...] + _dot(p.astype(BF16), v)
    m_scr[...] = m_new


def _attn_prompt_kernel(qi_ref, kj_ref, q_ref, k_ref, o_ref, qs_scr, m_scr, l_scr, acc_scr, *, tq, tk):
    p = pl.program_id(1)
    i = qi_ref[p]
    j = kj_ref[p]
    rows = MLA_HEADS * tq

    @pl.when(j == 0)
    def _():
        for h in range(MLA_HEADS):
            qs_scr[h * tq:(h + 1) * tq, :] = q_ref[:, h * KCAT:(h + 1) * KCAT]
        m_scr[...] = jnp.full((rows, LANES), -jnp.inf, F32)
        l_scr[...] = jnp.zeros((rows, LANES), F32)
        acc_scr[...] = jnp.zeros((rows, KV_LORA), F32)

    k = k_ref[...]
    s = _dot_nt(qs_scr[...], k)
    v = k[:, :KV_LORA]
    crosses = (j + 1) * tk - 1 > i * tq

    @pl.when(crosses)
    def _():
        qpos = (lax.broadcasted_iota(I32, (rows, tk), 0) & (tq - 1)) + i * tq
        kpos = lax.broadcasted_iota(I32, (rows, tk), 1) + j * tk
        _softmax_step(jnp.where(kpos <= qpos, s, -jnp.inf), v, m_scr, l_scr, acc_scr)

    @pl.when(jnp.logical_not(crosses))
    def _():
        _softmax_step(s, v, m_scr, l_scr, acc_scr)

    @pl.when(j == ((i + 1) * tq - 1) // tk)
    def _():
        inv = _lane_tile(1.0 / l_scr[...], KV_LORA // LANES)
        for h in range(MLA_HEADS):
            o_ref[:, h * KV_LORA:(h + 1) * KV_LORA] = (
                acc_scr[h * tq:(h + 1) * tq, :] * inv[h * tq:(h + 1) * tq, :]).astype(o_ref.dtype)


def _attn_prompt(qcat, kcat, bsz, t, tq, tk):
    nq, nk = t // tq, t // tk
    pairs = [(i, j) for i in range(nq) for j in range(((i + 1) * tq - 1) // tk + 1)]
    qi = jnp.asarray([p[0] for p in pairs], I32)
    kj = jnp.asarray([p[1] for p in pairs], I32)
    rows = MLA_HEADS * tq
    grid_spec = pltpu.PrefetchScalarGridSpec(
        num_scalar_prefetch=2,
        grid=(bsz, len(pairs)),
        in_specs=[pl.BlockSpec((tq, MLA_HEADS * KCAT), lambda b, p, qi, kj: (b * nq + qi[p], 0)),
                  pl.BlockSpec((tk, KCAT), lambda b, p, qi, kj: (b * nk + kj[p], 0))],
        out_specs=pl.BlockSpec((tq, MLA_HEADS * KV_LORA), lambda b, p, qi, kj: (b * nq + qi[p], 0)),
        scratch_shapes=[pltpu.VMEM((rows, KCAT), BF16), pltpu.VMEM((rows, LANES), F32),
                        pltpu.VMEM((rows, LANES), F32), pltpu.VMEM((rows, KV_LORA), F32)],
    )
    return pl.pallas_call(
        functools.partial(_attn_prompt_kernel, tq=tq, tk=tk),
        grid_spec=grid_spec,
        out_shape=jax.ShapeDtypeStruct((bsz * t, MLA_HEADS * KV_LORA), BF16),
        compiler_params=_cparams("parallel", "arbitrary"),
        name="attn_prompt",
    )(qi, kj, qcat, kcat)


def _attn_sample_kernel(pt_ref, q_ref, kn_ref, *rest, npages, t):
    lat_refs = rest[:npages]
    kr_refs = rest[npages:2 * npages]
    o_ref = rest[2 * npages]
    qs_scr, m_scr, l_scr, acc_scr = rest[2 * npages + 1:]
    step = pl.program_id(1)
    rows = MLA_HEADS * t

    @pl.when(step == 0)
    def _():
        for h in range(MLA_HEADS):
            qs_scr[h * t:(h + 1) * t, :] = q_ref[:, h * KCAT:(h + 1) * KCAT]
        m_scr[...] = jnp.full((rows, LANES), -jnp.inf, F32)
        l_scr[...] = jnp.zeros((rows, LANES), F32)
        acc_scr[...] = jnp.zeros((rows, KV_LORA), F32)

    qs = qs_scr[...].astype(BF16)
    klat = jnp.concatenate([r[...].astype(BF16) for r in lat_refs], axis=0)
    krope_t = jnp.concatenate([r[...].astype(BF16) for r in kr_refs], axis=1)
    s = _dot_nt(qs[:, :KV_LORA], klat) + _dot(qs[:, KV_LORA:KV_LORA + QK_ROPE], krope_t)
    _softmax_step(s, klat, m_scr, l_scr, acc_scr)

    @pl.when(step == pl.num_programs(1) - 1)
    def _():
        kn = jnp.concatenate([kn_ref[...], jnp.zeros((LANES - t, KCAT), F32)], axis=0).astype(BF16)
        sn = _dot_nt(qs, kn)
        qpos = lax.broadcasted_iota(I32, (rows, LANES), 0) & (t - 1)
        kpos = lax.broadcasted_iota(I32, (rows, LANES), 1)
        _softmax_step(jnp.where(kpos <= qpos, sn, -jnp.inf), kn[:, :KV_LORA], m_scr, l_scr, acc_scr)
        inv = _lane_tile(1.0 / l_scr[...], KV_LORA // LANES)
        for h in range(MLA_HEADS):
            o_ref[:, h * KV_LORA:(h + 1) * KV_LORA] = acc_scr[h * t:(h + 1) * t, :] * inv[h * t:(h + 1) * t, :]


def _attn_sample(qcat, kcat, cache_lat, cache_kr, page_table, layer, bsz, t, npages):
    n_pages = page_table.shape[1]
    nsteps = n_pages // npages

    def page_map(b, s, pt, *, i):
        return (layer, pt[b, s * npages + i], 0, 0)

    lat_specs = [pl.BlockSpec((None, None, PAGE_SIZE, KV_LORA), functools.partial(page_map, i=i)) for i in range(npages)]
    kr_specs = [pl.BlockSpec((None, None, QK_ROPE, PAGE_SIZE), functools.partial(page_map, i=i)) for i in range(npages)]
    rows = MLA_HEADS * t
    grid_spec = pltpu.PrefetchScalarGridSpec(
        num_scalar_prefetch=1,
        grid=(bsz, nsteps),
        in_specs=[pl.BlockSpec((None, t, MLA_HEADS * KCAT), lambda b, s, pt: (b, 0, 0)),
                  pl.BlockSpec((None, t, KCAT), lambda b, s, pt: (b, 0, 0))] + lat_specs + kr_specs,
        out_specs=pl.BlockSpec((None, t, MLA_HEADS * KV_LORA), lambda b, s, pt: (b, 0, 0)),
        scratch_shapes=[pltpu.VMEM((rows, KCAT), F32), pltpu.VMEM((rows, LANES), F32),
                        pltpu.VMEM((rows, LANES), F32), pltpu.VMEM((rows, KV_LORA), F32)],
    )
    out = pl.pallas_call(
        functools.partial(_attn_sample_kernel, npages=npages, t=t),
        grid_spec=grid_spec,
        out_shape=jax.ShapeDtypeStruct((bsz, t, MLA_HEADS * KV_LORA), F32),
        compiler_params=_cparams("parallel", "arbitrary"),
        name="attn_sample",
    )(page_table, qcat.reshape(bsz, t, -1), kcat.reshape(bsz, t, -1),
      *([cache_lat] * npages), *([jnp.swapaxes(cache_kr, 2, 3)] * npages))
    return out.reshape(bsz * t, -1)


def _mla_out_kernel(o_ref, wuv_ref, wo_ref, xr_ref, g_ref, b_ref, out_ref):
    o = o_ref[...].astype(BF16)
    parts = [_dot(o[:, h * KV_LORA:(h + 1) * KV_LORA], wuv_ref[h]).astype(BF16) for h in range(MLA_HEADS)]
    h = _dot(jnp.concatenate(parts, axis=1), wo_ref[...])
    out_ref[...] = _layer_norm(DEEPNORM_ALPHA * xr_ref[...] + h, g_ref[...], b_ref[...])


def _mla_out(o, wuv, wo, xres, g, b, tm):
    m = o.shape[0]
    full = lambda a: pl.BlockSpec(a.shape, lambda i: (0,) * a.ndim)
    row = lambda n: pl.BlockSpec((tm, n), lambda i: (i, 0))
    return pl.pallas_call(
        _mla_out_kernel,
        grid=(m // tm,),
        in_specs=[row(MLA_HEADS * KV_LORA), full(wuv), full(wo), row(D_MODEL), full(g), full(b)],
        out_specs=row(D_MODEL),
        out_shape=jax.ShapeDtypeStruct((m, D_MODEL), F32),
        compiler_params=_cparams("parallel"),
        name="mla_out",
    )(o, wuv, wo, xres, g, b)


def _route_kernel(x_ref, rh_ref, rl_ref, bias_ref, pa_ref, pb_ref, wa_ref, wb_ref, ng_ref, *, subs):
    tm = subs * MOE_SUB
    xh, xm = _split2(x_ref[...])
    rh = rh_ref[...]
    logits = _dot_nt(rh, xh) + _dot_nt(rh, xm) + _dot_nt(rl_ref[...], xh)
    aff = jax.nn.sigmoid(logits)
    sel = aff + bias_ref[...]
    sub = lax.broadcasted_iota(I32, (EXPERTS_PER_GROUP, tm), 0)
    for g in range(MOE_GROUPS):
        sl = slice(g * EXPERTS_PER_GROUP, (g + 1) * EXPERTS_PER_GROUP)
        sg, ag = sel[sl, :], aff[sl, :]
        m1 = jnp.max(sg, axis=0, keepdims=True)
        i1 = jnp.min(jnp.where(sg == m1, sub, EXPERTS_PER_GROUP), axis=0, keepdims=True)
        sg2 = jnp.where(sub == i1, -jnp.inf, sg)
        m2 = jnp.max(sg2, axis=0, keepdims=True)
        i2 = jnp.min(jnp.where(sg2 == m2, sub, EXPERTS_PER_GROUP), axis=0, keepdims=True)
        a1 = jnp.sum(jnp.where(sub == i1, ag, 0.0), axis=0, keepdims=True)
        a2 = jnp.sum(jnp.where(sub == i2, ag, 0.0), axis=0, keepdims=True)
        score = m1 + m2
        if g == 0:
            best, ea, eb, va, vb = score, i1, i2, a1, a2
        else:
            better = score > best
            best = jnp.where(better, score, best)
            ea = jnp.where(better, i1 + g * EXPERTS_PER_GROUP, ea)
            eb = jnp.where(better, i2 + g * EXPERTS_PER_GROUP, eb)
            va = jnp.where(better, a1, va)
            vb = jnp.where(better, a2, vb)
    den = va + vb
    wa_ref[...] = va / den
    wb_ref[...] = vb / den

    eid = lax.broadcasted_iota(I32, (N_EXPERTS, tm), 0)
    oh_a = jnp.where(eid == ea, 1.0, 0.0)
    oh_b = jnp.where(eid == eb, 1.0, 0.0)
    before = lax.broadcasted_iota(I32, (MOE_SUB, MOE_SUB), 0) < lax.broadcasted_iota(I32, (MOE_SUB, MOE_SUB), 1)
    upper = jnp.where(before, 1.0, 0.0).astype(BF16)
    lower = jnp.where(lax.broadcasted_iota(I32, (N_EXPERTS, N_EXPERTS), 1)
                      < lax.broadcasted_iota(I32, (N_EXPERTS, N_EXPERTS), 0), 1.0, 0.0).astype(BF16)
    lane = lax.broadcasted_iota(I32, (N_EXPERTS, LANES), 1)
    ng_all = jnp.zeros((N_EXPERTS, LANES), F32)
    pos_a, pos_b = [], []
    for s in range(subs):
        a = oh_a[:, s * MOE_SUB:(s + 1) * MOE_SUB]
        b = oh_b[:, s * MOE_SUB:(s + 1) * MOE_SUB]
        cnt_a = jnp.sum(a, axis=1, keepdims=True)
        cnt = cnt_a + jnp.sum(b, axis=1, keepdims=True)
        ng = jnp.floor((cnt + (MOE_GRAN - 1)) * (1.0 / MOE_GRAN))
        rows = jnp.broadcast_to(ng * MOE_GRAN, (N_EXPERTS, LANES)).astype(BF16)
        start = _dot(lower, rows)[:, :1]
        pos_a.append(jnp.sum(a * (start + _dot(a.astype(BF16), upper)), axis=0, keepdims=True))
        pos_b.append(jnp.sum(b * (start + cnt_a + _dot(b.astype(BF16), upper)), axis=0, keepdims=True))
        ng_all = jnp.where(lane == s, ng, ng_all)
    pa_ref[...] = jnp.concatenate(pos_a, axis=1).astype(I32)
    pb_ref[...] = jnp.concatenate(pos_b, axis=1).astype(I32)
    ng_ref[...] = ng_all


def _route(x, rh, rl, bias, subs):
    m = x.shape[0]
    tm = subs * MOE_SUB
    full = lambda a: pl.BlockSpec(a.shape, lambda i: (0, 0))
    lane_row = pl.BlockSpec((1, tm), lambda i: (0, i))
    row_i = jax.ShapeDtypeStruct((1, m), I32)
    row_f = jax.ShapeDtypeStruct((1, m), F32)
    return pl.pallas_call(
        functools.partial(_route_kernel, subs=subs),
        grid=(m // tm,),
        in_specs=[pl.BlockSpec((tm, D_MODEL), lambda i: (i, 0)), full(rh), full(rl), full(bias)],
        out_specs=[lane_row, lane_row, lane_row, lane_row,
                   pl.BlockSpec((None, N_EXPERTS, LANES), lambda i: (i, 0, 0))],
        out_shape=[row_i, row_i, row_f, row_f, jax.ShapeDtypeStruct((m // tm, N_EXPERTS, LANES), F32)],
        compiler_params=_cparams("parallel"),
        name="moe_route",
    )(x, rh, rl, bias)


def _moe_kernel(ng_ref, gs_ref, nit_ref, x_ref, pa_ref, pb_ref, wa_ref, wb_ref, wg_ref, wu_ref, wd_ref,
                g_ref, b_ref, o_ref, buf, *, ns, subs, eps):
    c = pl.program_id(0)
    st = pl.program_id(1)
    n_sort = ns // subs
    n_exp = N_EXPERTS // eps
    row_id = lax.broadcasted_iota(I32, (MOE_ROWS, MOE_SUB), 0)

    @pl.when(st < n_sort)
    def _():
        for k in range(subs):
            sl = slice(k * MOE_SUB, (k + 1) * MOE_SUB)
            hit = jnp.logical_or(row_id == pa_ref[:, sl], row_id == pb_ref[:, sl])
            perm = jnp.where(hit, 1.0, 0.0).astype(BF16)
            s = st * subs + k
            buf[s, 0:MOE_ROWS, :] = _dot(perm, x_ref[sl, :].astype(BF16)).astype(BF16)
            buf[s, MOE_ROWS:MOE_ROWS + MOE_GRAN, :] = jnp.zeros((MOE_GRAN, D_MODEL), BF16)

    @pl.when(jnp.logical_and(st >= n_sort, st < n_sort + n_exp))
    def _():
        e0 = (st - n_sort) * eps

        def body(w, carry):
            starts, xws = [], []
            for j in range(eps):
                row0s = []
                for s in range(ns):
                    idx = (c * ns + s) * N_EXPERTS + e0 + j
                    row0 = jnp.where(w < ng_ref[idx], gs_ref[idx] + w * MOE_GRAN, MOE_ROWS)
                    row0s.append(pl.multiple_of(row0, MOE_GRAN))
                starts.append(row0s)
                xws.append(jnp.concatenate([buf[s, pl.ds(row0s[s], MOE_GRAN), :] for s in range(ns)], axis=0))
            ys = []
            for j in range(eps):
                h = _silu(_dot(xws[j], wg_ref[j])) * _dot(xws[j], wu_ref[j])
                ys.append(_dot(h.astype(BF16), wd_ref[j]).astype(BF16))
            for j in range(eps):
                for s in range(ns):
                    buf[s, pl.ds(starts[j][s], MOE_GRAN), :] = ys[j][s * MOE_GRAN:(s + 1) * MOE_GRAN, :]
            return carry

        trips = nit_ref[c * N_EXPERTS + e0]
        for j in range(1, eps):
            trips = jnp.maximum(trips, nit_ref[c * N_EXPERTS + e0 + j])
        lax.fori_loop(0, trips, body, 0)

    @pl.when(st >= n_sort + n_exp)
    def _():
        for k in range(subs):
            sl = slice(k * MOE_SUB, (k + 1) * MOE_SUB)
            gate = (jnp.where(row_id == pa_ref[:, sl], wa_ref[:, sl], 0.0)
                    + jnp.where(row_id == pb_ref[:, sl], wb_ref[:, sl], 0.0)).astype(BF16)
            s = (st - n_sort - n_exp) * subs + k
            y = _dot_tn(gate, buf[s, 0:MOE_ROWS, :])
            o_ref[sl, :] = _layer_norm(DEEPNORM_ALPHA * x_ref[sl, :] + y, g_ref[...], b_ref[...])


def _hier_moe_ln(x, prm, g, b, ns):
    m = x.shape[0]
    subs, eps = MOE_SUBS_PER_STEP, MOE_EXPERTS_PER_STEP
    tm = subs * MOE_SUB
    chunk = ns * MOE_SUB
    n_chunks = m // chunk
    n_sort = ns // subs
    n_exp = N_EXPERTS // eps
    pa, pb, wa, wb, ng = _route(x, prm["router_hi"], prm["router_lo"], prm["router_bias"], subs)
    ng = jnp.transpose(ng[:, :, :subs].astype(I32), (0, 2, 1)).reshape(n_chunks, ns, N_EXPERTS)
    gstart = (jnp.cumsum(ng, axis=-1) - ng) * MOE_GRAN
    nit = jnp.max(ng, axis=1)

    def tok_idx(c, st, *_):
        phase = jnp.where(st < n_sort, st, jnp.where(st >= n_sort + n_exp, st - n_sort - n_exp, n_sort - 1))
        return c * n_sort + phase

    def exp_idx(c, st, *_):
        return (jnp.clip(st - n_sort, 0, n_exp - 1), 0, 0)

    lane_row = pl.BlockSpec((1, tm), lambda c, st, *_: (0, tok_idx(c, st)))
    full = lambda a: pl.BlockSpec(a.shape, lambda c, st, *_: (0, 0))
    grid_spec = pltpu.PrefetchScalarGridSpec(
        num_scalar_prefetch=3,
        grid=(n_chunks, 2 * n_sort + n_exp),
        in_specs=[pl.BlockSpec((tm, D_MODEL), lambda c, st, *_: (tok_idx(c, st), 0)),
                  lane_row, lane_row, lane_row, lane_row,
                  pl.BlockSpec((eps, D_MODEL, EXPERT_FF), exp_idx),
                  pl.BlockSpec((eps, D_MODEL, EXPERT_FF), exp_idx),
                  pl.BlockSpec((eps, EXPERT_FF, D_MODEL), exp_idx),
                  full(g), full(b)],
        out_specs=pl.BlockSpec((tm, D_MODEL),
                               lambda c, st, *_: (c * n_sort + jnp.clip(st - n_sort - n_exp, 0, n_sort - 1), 0)),
        scratch_shapes=[pltpu.VMEM((ns, MOE_ROWS + MOE_GRAN, D_MODEL), BF16)],
    )
    return pl.pallas_call(
        functools.partial(_moe_kernel, ns=ns, subs=subs, eps=eps),
        grid_spec=grid_spec,
        out_shape=jax.ShapeDtypeStruct((m, D_MODEL), F32),
        compiler_params=_cparams("arbitrary", "arbitrary"),
        name="moe_experts",
    )(ng.reshape(-1), gstart.reshape(-1), nit.reshape(-1), x, pa, pb, wa, wb,
      prm["w_gate"], prm["w_up"], prm["w_down"], g, b)


def _pad_cols(a, n):
    return jnp.pad(a, ((0, 0), (0, n - a.shape[1])))


def _ssd_params(w_in, conv_w, conv_b, dt_bias, a_log, d_skip, norm_w, w_out):
    wdt = _pad_cols(w_in[:, D_INNER + CONV_DIM:], LANES)
    wdh = wdt.astype(BF16)
    head_of_col = jnp.arange(D_INNER, dtype=I32) // SSD_HEAD_DIM
    expand = (jnp.arange(LANES, dtype=I32)[:, None] == head_of_col[None, :]).astype(BF16)
    return {
        "wz": w_in[:, :D_INNER].astype(BF16),
        "wx": w_in[:, D_INNER:D_INNER + CONV_DIM].astype(BF16),
        "wdh": wdh,
        "wdl": (wdt - wdh.astype(F32)).astype(BF16),
        "conv_w": conv_w,
        "conv_b": conv_b[None, :],
        "dt_bias": _pad_cols(dt_bias[None, :], LANES),
        "a_log": _pad_cols(a_log[None, :], LANES),
        "d_skip": jnp.repeat(d_skip, SSD_HEAD_DIM)[None, :],
        "norm_w": norm_w[None, :],
        "expand": expand,
        "w_out": w_out.astype(BF16),
    }


def _mla_params(w_dqkv, q_norm, kv_norm, w_uq, w_uk, w_uv, w_o):
    wr = jnp.pad(w_uq[:, :, QK_NOPE:], ((0, 0), (0, 0), (0, LANES - QK_ROPE)))
    return {
        "w_down": _pad_cols(w_dqkv, DCOLS).astype(BF16),
        "q_norm": q_norm[None, :],
        "kv_norm": kv_norm[None, :],
        "w_qn": w_uq[:, :, :QK_NOPE].reshape(Q_LORA, MLA_HEADS * QK_NOPE).astype(BF16),
        "w_qr": wr.reshape(Q_LORA, MLA_HEADS * LANES).astype(BF16),
        "w_uk_t": jnp.transpose(w_uk, (1, 2, 0)).astype(BF16),
        "w_uv": jnp.transpose(w_uv, (1, 0, 2)).astype(BF16),
        "w_o": w_o.astype(BF16),
    }


def _moe_params(router, router_bias, w_gate, w_up, w_down):
    rt = router.T
    rh = rt.astype(BF16)
    return {
        "router_hi": rh,
        "router_lo": (rt - rh.astype(F32)).astype(BF16),
        "router_bias": router_bias[:, None],
        "w_gate": w_gate.astype(BF16),
        "w_up": w_up.astype(BF16),
        "w_down": w_down.astype(BF16),
    }


def _rope_tables(pos):
    half = QK_ROPE // 2
    inv_freq = ROPE_THETA ** (-jnp.arange(half, dtype=F32) / half)
    ang = pos.astype(F32)[:, None] * inv_freq[None, :]
    cos, sin = jnp.cos(ang), jnp.sin(ang)
    zeros = jnp.zeros((pos.shape[0], LANES - QK_ROPE), F32)
    return jnp.concatenate([cos, cos, zeros], axis=1), jnp.concatenate([-sin, sin, zeros], axis=1)


def _tile(m, want):
    return min(m, want)


def kernel(x_prompt, x_sample, cache_mla_latent, cache_mla_krope, state_ssd, state_conv, page_table,
           ln_mix_g, ln_mix_b, ln_ffn_g, ln_ffn_b,
           ssd_w_in, ssd_conv_w, ssd_conv_b, ssd_dt_bias, ssd_a_log, ssd_d_skip, ssd_norm_w, ssd_w_out,
           mla_w_dqkv, mla_q_norm, mla_kv_norm, mla_w_uq, mla_w_uk, mla_w_uv, mla_w_o,
           moe_router, moe_router_bias, moe_w_gate, moe_w_up, moe_w_down):
    bp, tp, _ = x_prompt.shape
    bs, ts, _ = x_sample.shape
    past_len = page_table.shape[1] * PAGE_SIZE
    xp = x_prompt.reshape(bp * tp, D_MODEL)
    xs = x_sample.reshape(bs * ts, D_MODEL)
    lp = min(SSD_CHUNK, tp)

    cos_p, sin_p = _rope_tables(jnp.arange(tp))
    cos_s, sin_s = _rope_tables(past_len + jnp.arange(ts))
    cos_s = jnp.tile(cos_s, (bs, 1))
    sin_s = jnp.tile(sin_s, (bs, 1))

    p_lat, p_kr, p_ssm, p_conv = [], [], [], []
    s_lat, s_kr, s_ssm, s_conv = [], [], [], []
    for i in range(DEPTH):
        j = i // 2
        g_mix, b_mix = ln_mix_g[i][None, :], ln_mix_b[i][None, :]
        if i % 2 == 0:
            prm = _ssd_params(ssd_w_in[j], ssd_conv_w[j], ssd_conv_b[j], ssd_dt_bias[j], ssd_a_log[j],
                              ssd_d_skip[j], ssd_norm_w[j], ssd_w_out[j])
            z, xbc, dt = _ssd_inproj(xp, prm["wz"], prm["wx"], prm["wdh"], prm["wdl"], BF16, _tile(xp.shape[0], 256))
            y, cp, sp = _ssd_core(xbc, z, dt, prm, bp, tp, lp, BF16)
            xp = _proj_ln(y, prm["w_out"], xp, g_mix, b_mix, _tile(xp.shape[0], 512))
            z, xbc, dt = _ssd_inproj(xs, prm["wz"], prm["wx"], prm["wdh"], prm["wdl"], F32, _tile(xs.shape[0], 256))
            y, cs, ss = _ssd_core(xbc, z, dt, prm, bs, ts, ts, F32, init=(state_conv, state_ssd, j))
            xs = _proj_ln(y, prm["w_out"], xs, g_mix, b_mix, _tile(xs.shape[0], 512))
            p_conv.append(cp)
            p_ssm.append(sp)
            s_conv.append(cs)
            s_ssm.append(ss)
        else:
            prm = _mla_params(mla_w_dqkv[j], mla_q_norm[j], mla_kv_norm[j], mla_w_uq[j], mla_w_uk[j],
                              mla_w_uv[j], mla_w_o[j])
            tm = _tile(tp, 512)
            cq, lat, kr, kcat = _mla_down(xp, prm["w_down"], prm["q_norm"], prm["kv_norm"], cos_p, sin_p, BF16, tm)
            qcat = _mla_q(cq, prm["w_qn"], prm["w_qr"], prm["w_uk_t"], cos_p, sin_p, BF16, tm)
            o = _attn_prompt(qcat, kcat, bp, tp, tm, tm)
            xp = _mla_out(o, prm["w_uv"], prm["w_o"], xp, g_mix, b_mix, tm)
            p_lat.append(lat.reshape(bp, tp, KV_LORA))
            p_kr.append(kr.reshape(bp, tp, QK_ROPE))
            tm = _tile(xs.shape[0], 512)
            cq, lat, kr, kcat = _mla_down(xs, prm["w_down"], prm["q_norm"], prm["kv_norm"], cos_s, sin_s, F32, tm)
            qcat = _mla_q(cq, prm["w_qn"], prm["w_qr"], prm["w_uk_t"], cos_s, sin_s, F32, tm)
            o = _attn_sample(qcat, kcat, cache_mla_latent, cache_mla_krope, page_table, j, bs, ts,
                             min(64, page_table.shape[1]))
            xs = _mla_out(o, prm["w_uv"], prm["w_o"], xs, g_mix, b_mix, tm)
            s_lat.append(lat.reshape(bs, ts, KV_LORA))
            s_kr.append(kr.reshape(bs, ts, QK_ROPE))
        mprm = _moe_params(moe_router[i], moe_router_bias[i], moe_w_gate[i], moe_w_up[i], moe_w_down[i])
        g_ffn, b_ffn = ln_ffn_g[i][None, :], ln_ffn_b[i][None, :]
        xp = _hier_moe_ln(xp, mprm, g_ffn, b_ffn, min(8, xp.shape[0] // MOE_SUB))
        xs = _hier_moe_ln(xs, mprm, g_ffn, b_ffn, min(8, xs.shape[0] // MOE_SUB))

    return (xp.reshape(bp, tp, D_MODEL), xs.reshape(bs, ts, D_MODEL),
            jnp.stack(p_lat), jnp.stack(p_kr), jnp.stack(p_ssm), jnp.stack(p_conv),
            jnp.stack(s_lat), jnp.stack(s_kr), jnp.stack(s_ssm), jnp.stack(s_conv))
```

```python
import functools

import jax
import jax.numpy as jnp
from jax import lax
from jax.experimental import pallas as pl
from jax.experimental.pallas import tpu as pltpu

F32 = jnp.float32
BF16 = jnp.bfloat16
I32 = jnp.int32

D_MODEL = 1024
DEPTH = 4
PAGE_SIZE = 128
D_INNER = 2048
SSD_HEAD_DIM = 64
SSD_HEADS = 32
SSD_GROUPS = 4
SSD_STATE = 128
SSD_CONV = 4
SSD_CHUNK = 128
CONV_DIM = D_INNER + 2 * SSD_GROUPS * SSD_STATE
GROUP_INNER = D_INNER // SSD_GROUPS
MLA_HEADS = 8
Q_LORA = 384
KV_LORA = 256
QK_NOPE = 128
QK_ROPE = 64
V_HEAD = 128
ROPE_THETA = 10000.0
MOE_GROUPS = 4
EXPERTS_PER_GROUP = 8
N_EXPERTS = 32
EXPERT_FF = 256
DEEPNORM_ALPHA = (2.0 * DEPTH) ** 0.25
LN_EPS = 1e-5
RMS_EPS = 1e-6
ATTN_SCALE = (QK_NOPE + QK_ROPE) ** -0.5

LANES = 128
KCAT = KV_LORA + LANES
DCOLS = Q_LORA + KV_LORA + LANES
MOE_SUB = 256
MOE_GRAN = 16
MOE_ROWS = 2 * MOE_SUB + N_EXPERTS * MOE_GRAN
MOE_SUBS_PER_STEP = 2
MOE_EXPERTS_PER_STEP = 4
VMEM_LIMIT = 56 * 1024 * 1024


def _cparams(*sem):
    return pltpu.CompilerParams(dimension_semantics=sem, vmem_limit_bytes=VMEM_LIMIT)


def _dot(a, b):
    return jnp.dot(a, b, preferred_element_type=F32)


def _dot_nt(a, b):
    return lax.dot_general(a, b, (((1,), (1,)), ((), ())), preferred_element_type=F32)


def _dot_tn(a, b):
    return lax.dot_general(a, b, (((0,), (0,)), ((), ())), preferred_element_type=F32)


def _split2(x):
    hi = x.astype(BF16)
    return hi, (x - hi.astype(F32)).astype(BF16)


def _split3(x):
    hi = x.astype(BF16)
    r = x - hi.astype(F32)
    mid = r.astype(BF16)
    return hi, mid, (r - mid.astype(F32)).astype(BF16)


def _silu(x):
    return x * jax.nn.sigmoid(x)


def _softplus(x):
    return jnp.maximum(x, 0.0) + jnp.log(1.0 + jnp.exp(-jnp.abs(x)))


def _layer_norm(v, g, b):
    mu = jnp.mean(v, axis=-1, keepdims=True)
    d = v - mu
    var = jnp.mean(d * d, axis=-1, keepdims=True)
    return d * lax.rsqrt(var + LN_EPS) * g + b


def _rms_norm(v, g):
    return v * lax.rsqrt(jnp.mean(v * v, axis=-1, keepdims=True) + RMS_EPS) * g


def _rope128(x, cos, sin):
    lane = lax.broadcasted_iota(I32, x.shape, 1)
    partner = jnp.where(lane < QK_ROPE // 2, pltpu.roll(x, LANES - QK_ROPE // 2, 1), pltpu.roll(x, QK_ROPE // 2, 1))
    return x * cos + partner * sin


def _ssd_inproj_kernel(x_ref, wz_ref, wx_ref, wdh_ref, wdl_ref, z_ref, xbc_ref, dt_ref):
    x = x_ref[...]
    xh, xm = _split2(x)
    z_ref[...] = _dot(xh, wz_ref[...]).astype(z_ref.dtype)
    xbc_ref[...] = _dot(xh, wx_ref[...])
    wdh = wdh_ref[...]
    dt_ref[...] = _dot(xh, wdh) + _dot(xm, wdh) + _dot(xh, wdl_ref[...])


def _ssd_inproj(x, wz, wx, wdh, wdl, z_dtype, tm):
    m = x.shape[0]
    full = lambda a: pl.BlockSpec(a.shape, lambda i: (0, 0))
    row = lambda n: pl.BlockSpec((tm, n), lambda i: (i, 0))
    return pl.pallas_call(
        _ssd_inproj_kernel,
        grid=(m // tm,),
        in_specs=[row(D_MODEL), full(wz), full(wx), full(wdh), full(wdl)],
        out_specs=[row(D_INNER), row(CONV_DIM), row(LANES)],
        out_shape=[jax.ShapeDtypeStruct((m, D_INNER), z_dtype),
                   jax.ShapeDtypeStruct((m, CONV_DIM), F32),
                   jax.ShapeDtypeStruct((m, LANES), F32)],
        compiler_params=_cparams("parallel"),
        name="ssd_inproj",
    )(x, wz, wx, wdh, wdl)


def _ssd_kernel(*refs, L, has_init):
    if has_init:
        (xbc_ref, z_ref, dt_ref, cw_ref, cb_ref, dtb_ref, alog_ref, dskip_ref, nw_ref, e_ref,
         conv0_ref, ssm0_ref, y_ref, convo_ref, sso_ref, xp_scr, h_scr) = refs
    else:
        (xbc_ref, z_ref, dt_ref, cw_ref, cb_ref, dtb_ref, alog_ref, dskip_ref, nw_ref, e_ref,
         y_ref, convo_ref, sso_ref, xp_scr, h_scr) = refs
    c = pl.program_id(1)
    nc = pl.num_programs(1)
    LK = max(L, LANES)
    K1 = SSD_CONV - 1
    HPG = SSD_HEADS // SSD_GROUPS

    @pl.when(c == 0)
    def _():
        if has_init:
            xp_scr[8 - K1:8, :] = conv0_ref[...]
            h_scr[...] = ssm0_ref[...].reshape(D_INNER, SSD_STATE)
        else:
            xp_scr[0:8, :] = jnp.zeros((8, CONV_DIM), F32)
            h_scr[...] = jnp.zeros((D_INNER, SSD_STATE), F32)

    def pad_rows(a):
        if L == LK:
            return a
        return jnp.concatenate([a, jnp.zeros((LK - L, a.shape[1]), a.dtype)], axis=0)

    xraw = xbc_ref[...]
    xp_scr[8:8 + L, :] = xraw
    acc = cb_ref[...] + xp_scr[5:5 + L, :] * cw_ref[0:1, :]
    acc = acc + xp_scr[6:6 + L, :] * cw_ref[1:2, :]
    acc = acc + xp_scr[7:7 + L, :] * cw_ref[2:3, :]
    acc = acc + xraw * cw_ref[3:4, :]
    conv = _silu(acc)
    tail = xp_scr[L + 8 - K1:L + 8, :]

    @pl.when(c == nc - 1)
    def _():
        convo_ref[...] = tail

    xp_scr[8 - K1:8, :] = tail

    xs = conv[:, :D_INNER]
    bm = conv[:, D_INNER:D_INNER + SSD_GROUPS * SSD_STATE]
    cm = conv[:, D_INNER + SSD_GROUPS * SSD_STATE:]

    dtv = _softplus(dt_ref[...] + dtb_ref[...])
    d_a = dtv * (-jnp.exp(alog_ref[...]))

    row = lax.broadcasted_iota(I32, (L, LK), 0)
    col = lax.broadcasted_iota(I32, (L, LK), 1)
    causal = col <= row
    tril = jnp.where(causal, 1.0, 0.0).astype(BF16)
    cumc = sum(_dot(tril, p) for p in _split3(pad_rows(d_a)))
    ir = lax.broadcasted_iota(I32, (LANES, LANES), 0)
    ic = lax.broadcasted_iota(I32, (LANES, LANES), 1)
    ident = jnp.where(ir == ic, 1.0, 0.0).astype(BF16)
    cumr = sum(_dot_nt(ident, p) for p in _split3(pad_rows(cumc)))

    ecum = jnp.exp(cumc)
    last = cumc[L - 1:L, :]
    ws = jnp.exp(last - cumc)
    elc = jnp.exp(cumr[:, L - 1:L])
    e_mat = e_ref[...]
    dt_x = _dot(dtv.astype(BF16), e_mat)
    ecum_x = _dot(ecum.astype(BF16), e_mat)
    ws_x = _dot(ws.astype(BF16), e_mat)

    xdt = xs * dt_x
    xdt_b = xdt.astype(BF16)
    xw_p = pad_rows((xdt * ws_x).astype(BF16))
    lo_half = lax.broadcasted_iota(I32, (LK, LANES), 1) < SSD_HEAD_DIM
    zero_b = jnp.zeros((LK, LANES), BF16)
    dskip = dskip_ref[...]
    neg_inf = jnp.float32(-jnp.inf)

    ys = []
    for g in range(SSD_GROUPS):
        bc_g = bm[:, g * SSD_STATE:(g + 1) * SSD_STATE].astype(BF16)
        cc_g = cm[:, g * SSD_STATE:(g + 1) * SSD_STATE].astype(BF16)
        bc_p = pad_rows(bc_g)
        cb = _dot_nt(cc_g, bc_p)
        h_g = h_scr[g * GROUP_INNER:(g + 1) * GROUP_INNER, :]
        ys_g = _dot_nt(cc_g, h_g.astype(BF16))
        for q in range(HPG // 2):
            ha = g * HPG + 2 * q
            c0 = (g * (HPG // 2) + q) * LANES
            dec = []
            for hh in (ha, ha + 1):
                seg = cumc[:, hh:hh + 1] - cumr[hh:hh + 1, :]
                dec.append(cb * jnp.exp(jnp.where(causal, seg, neg_inf)))
            m_cat = jnp.concatenate(dec, axis=1).astype(BF16)
            x_p = pad_rows(xdt_b[:, c0:c0 + LANES])
            xbd = jnp.concatenate([jnp.where(lo_half, x_p, zero_b), jnp.where(lo_half, zero_b, x_p)], axis=0)
            y_p = _dot(m_cat, xbd)
            y_p = y_p + ys_g[:, q * LANES:(q + 1) * LANES] * ecum_x[:, c0:c0 + LANES]
            y_p = y_p + dskip[:, c0:c0 + LANES] * xs[:, c0:c0 + LANES]
            ys.append(y_p)
        upd = _dot_tn(xw_p[:, g * GROUP_INNER:(g + 1) * GROUP_INNER], bc_p)
        scaled = [h_g[j * SSD_HEAD_DIM:(j + 1) * SSD_HEAD_DIM, :] * elc[g * HPG + j:g * HPG + j + 1, :]
                  for j in range(HPG)]
        h_scr[g * GROUP_INNER:(g + 1) * GROUP_INNER, :] = jnp.concatenate(scaled, axis=0) + upd

    y = jnp.concatenate(ys, axis=1)
    gated = y * _silu(z_ref[...].astype(F32))
    outs = []
    for g in range(SSD_GROUPS):
        gg = gated[:, g * GROUP_INNER:(g + 1) * GROUP_INNER]
        outs.append(gg * lax.rsqrt(jnp.mean(gg * gg, axis=-1, keepdims=True) + RMS_EPS))
    y_ref[...] = (jnp.concatenate(outs, axis=1) * nw_ref[...]).astype(y_ref.dtype)

    @pl.when(c == nc - 1)
    def _():
        sso_ref[...] = h_scr[...].reshape(SSD_HEADS, SSD_HEAD_DIM, SSD_STATE)


def _ssd_core(xbc, z, dt, prm, bsz, t, L, y_dtype, init=None):
    nc = t // L
    m = bsz * t
    has_init = init is not None
    tok = lambda n: pl.BlockSpec((L, n), lambda b, c: (b * nc + c, 0))
    full = lambda a: pl.BlockSpec(a.shape, lambda b, c: (0, 0))
    params = [prm["conv_w"], prm["conv_b"], prm["dt_bias"], prm["a_log"], prm["d_skip"], prm["norm_w"], prm["expand"]]
    in_specs = [tok(CONV_DIM), tok(D_INNER), tok(LANES)] + [full(a) for a in params]
    args = [xbc, z, dt] + params
    if has_init:
        conv0, ssm0, layer = init
        in_specs += [pl.BlockSpec((None, None, SSD_CONV - 1, CONV_DIM), lambda b, c: (layer, b, 0, 0)),
                     pl.BlockSpec((None, None, SSD_HEADS, SSD_HEAD_DIM, SSD_STATE), lambda b, c: (layer, b, 0, 0, 0))]
        args += [conv0, ssm0]
    return pl.pallas_call(
        functools.partial(_ssd_kernel, L=L, has_init=has_init),
        grid=(bsz, nc),
        in_specs=in_specs,
        out_specs=[tok(D_INNER),
                   pl.BlockSpec((None, SSD_CONV - 1, CONV_DIM), lambda b, c: (b, 0, 0)),
                   pl.BlockSpec((None, SSD_HEADS, SSD_HEAD_DIM, SSD_STATE), lambda b, c: (b, 0, 0, 0))],
        out_shape=[jax.ShapeDtypeStruct((m, D_INNER), y_dtype),
                   jax.ShapeDtypeStruct((bsz, SSD_CONV - 1, CONV_DIM), F32),
                   jax.ShapeDtypeStruct((bsz, SSD_HEADS, SSD_HEAD_DIM, SSD_STATE), F32)],
        scratch_shapes=[pltpu.VMEM((L + 8, CONV_DIM), F32), pltpu.VMEM((D_INNER, SSD_STATE), F32)],
        compiler_params=_cparams("parallel", "arbitrary"),
        name="ssd_core_init" if has_init else "ssd_core",
    )(*args)


def _proj_ln_kernel(y_ref, w_ref, xr_ref, g_ref, b_ref, o_ref):
    h = _dot(y_ref[...].astype(BF16), w_ref[...])
    o_ref[...] = _layer_norm(DEEPNORM_ALPHA * xr_ref[...] + h, g_ref[...], b_ref[...])


def _proj_ln(y, w, xres, g, b, tm):
    m, k = y.shape
    full = lambda a: pl.BlockSpec(a.shape, lambda i: (0, 0))
    row = lambda n: pl.BlockSpec((tm, n), lambda i: (i, 0))
    return pl.pallas_call(
        _proj_ln_kernel,
        grid=(m // tm,),
        in_specs=[row(k), full(w), row(D_MODEL), full(g), full(b)],
        out_specs=row(D_MODEL),
        out_shape=jax.ShapeDtypeStruct((m, D_MODEL), F32),
        compiler_params=_cparams("parallel"),
        name="proj_ln",
    )(y, w, xres, g, b)


def _mla_down_kernel(x_ref, w_ref, qn_ref, kvn_ref, cos_ref, sin_ref, cq_ref, ckv_ref, kr_ref, kcat_ref):
    c = _dot(x_ref[...].astype(BF16), w_ref[...])
    cq = _rms_norm(c[:, :Q_LORA], qn_ref[...])
    ckv = _rms_norm(c[:, Q_LORA:Q_LORA + KV_LORA], kvn_ref[...])
    kr = _rope128(c[:, Q_LORA + KV_LORA:], cos_ref[...], sin_ref[...])
    cq_ref[...] = cq.astype(BF16)
    ckv_ref[...] = ckv
    kr_ref[...] = kr[:, :QK_ROPE]
    kcat_ref[...] = jnp.concatenate([ckv, kr], axis=1).astype(kcat_ref.dtype)


def _mla_down(x, w, qn, kvn, cos, sin, kcat_dtype, tm):
    m = x.shape[0]
    nper = cos.shape[0] // tm
    full = lambda a: pl.BlockSpec(a.shape, lambda i: (0, 0))
    row = lambda n: pl.BlockSpec((tm, n), lambda i: (i, 0))
    tab = pl.BlockSpec((tm, LANES), lambda i: (i % nper, 0))
    return pl.pallas_call(
        _mla_down_kernel,
        grid=(m // tm,),
        in_specs=[row(D_MODEL), full(w), full(qn), full(kvn), tab, tab],
        out_specs=[row(Q_LORA), row(KV_LORA), row(QK_ROPE), row(KCAT)],
        out_shape=[jax.ShapeDtypeStruct((m, Q_LORA), BF16),
                   jax.ShapeDtypeStruct((m, KV_LORA), F32),
                   jax.ShapeDtypeStruct((m, QK_ROPE), F32),
                   jax.ShapeDtypeStruct((m, KCAT), kcat_dtype)],
        compiler_params=_cparams("parallel"),
        name="mla_down",
    )(x, w, qn, kvn, cos, sin)


def _mla_q_kernel(cq_ref, wn_ref, wr_ref, wuk_ref, cos_ref, sin_ref, q_ref):
    cq = cq_ref[...]
    qn = _dot(cq, wn_ref[...]).astype(BF16)
    qr = _dot(cq, wr_ref[...])
    cos = cos_ref[...]
    sin = sin_ref[...]
    for h in range(MLA_HEADS):
        ql = _dot(qn[:, h * QK_NOPE:(h + 1) * QK_NOPE], wuk_ref[h]) * ATTN_SCALE
        rp = _rope128(qr[:, h * LANES:(h + 1) * LANES], cos, sin) * ATTN_SCALE
        q_ref[:, h * KCAT:h * KCAT + KV_LORA] = ql.astype(q_ref.dtype)
        q_ref[:, h * KCAT + KV_LORA:(h + 1) * KCAT] = rp.astype(q_ref.dtype)


def _mla_q(cq, wn, wr, wuk, cos, sin, q_dtype, tm):
    m = cq.shape[0]
    nper = cos.shape[0] // tm
    full2 = lambda a: pl.BlockSpec(a.shape, lambda i: (0, 0))
    tab = pl.BlockSpec((tm, LANES), lambda i: (i % nper, 0))
    return pl.pallas_call(
        _mla_q_kernel,
        grid=(m // tm,),
        in_specs=[pl.BlockSpec((tm, Q_LORA), lambda i: (i, 0)), full2(wn), full2(wr),
                  pl.BlockSpec(wuk.shape, lambda i: (0, 0, 0)), tab, tab],
        out_specs=pl.BlockSpec((tm, MLA_HEADS * KCAT), lambda i: (i, 0)),
        out_shape=jax.ShapeDtypeStruct((m, MLA_HEADS * KCAT), q_dtype),
        compiler_params=_cparams("parallel"),
        name="mla_q",
    )(cq, wn, wr, wuk, cos, sin)


def _lane_tile(a, n):
    return a if n == 1 else jnp.concatenate([a] * n, axis=1)


def _softmax_step(s, v, m_scr, l_scr, acc_scr):
    m_prev = m_scr[...]
    m_new = jnp.maximum(m_prev, jnp.max(s, axis=-1, keepdims=True))
    corr = jnp.exp(m_prev - m_new)
    p = jnp.exp(s - _lane_tile(m_new, s.shape[1] // LANES))
    l_scr[...] = corr * l_scr[...] + jnp.sum(p, axis=-1, keepdims=True)
    acc_scr[...] = _lane_tile(corr, KV_LORA // LANES) * acc_scr[...] + _dot(p.astype(BF16), v)
    m_scr[...] = m_new


def _attn_prompt_kernel(qi_ref, kj_ref, q_ref, k_ref, o_ref, qs_scr, m_scr, l_scr, acc_scr, *, tq, tk):
    p = pl.program_id(1)
    i = qi_ref[p]
    j = kj_ref[p]
    rows = MLA_HEADS * tq

    @pl.when(j == 0)
    def _():
        for h in range(MLA_HEADS):
            qs_scr[h * tq:(h + 1) * tq, :] = q_ref[:, h * KCAT:(h + 1) * KCAT]
        m_scr[...] = jnp.full((rows, LANES), -jnp.inf, F32)
        l_scr[...] = jnp.zeros((rows, LANES), F32)
        acc_scr[...] = jnp.zeros((rows, KV_LORA), F32)

    k = k_ref[...]
    s = _dot_nt(qs_scr[...], k)
    v = k[:, :KV_LORA]
    crosses = (j + 1) * tk - 1 > i * tq

    @pl.when(crosses)
    def _():
        qpos = (lax.broadcasted_iota(I32, (rows, tk), 0) & (tq - 1)) + i * tq
        kpos = lax.broadcasted_iota(I32, (rows, tk), 1) + j * tk
        _softmax_step(jnp.where(kpos <= qpos, s, -jnp.inf), v, m_scr, l_scr, acc_scr)

    @pl.when(jnp.logical_not(crosses))
    def _():
        _softmax_step(s, v, m_scr, l_scr, acc_scr)

    @pl.when(j == ((i + 1) * tq - 1) // tk)
    def _():
        inv = _lane_tile(1.0 / l_scr[...], KV_LORA // LANES)
        for h in range(MLA_HEADS):
            o_ref[:, h * KV_LORA:(h + 1) * KV_LORA] = (
                acc_scr[h * tq:(h + 1) * tq, :] * inv[h * tq:(h + 1) * tq, :]).astype(o_ref.dtype)


def _attn_prompt(qcat, kcat, bsz, t, tq, tk):
    nq, nk = t // tq, t // tk
    pairs = [(i, j) for i in range(nq) for j in range(((i + 1) * tq - 1) // tk + 1)]
    qi = jnp.asarray([p[0] for p in pairs], I32)
    kj = jnp.asarray([p[1] for p in pairs], I32)
    rows = MLA_HEADS * tq
    grid_spec = pltpu.PrefetchScalarGridSpec(
        num_scalar_prefetch=2,
        grid=(bsz, len(pairs)),
        in_specs=[pl.BlockSpec((tq, MLA_HEADS * KCAT), lambda b, p, qi, kj: (b * nq + qi[p], 0)),
                  pl.BlockSpec((tk, KCAT), lambda b, p, qi, kj: (b * nk + kj[p], 0))],
        out_specs=pl.BlockSpec((tq, MLA_HEADS * KV_LORA), lambda b, p, qi, kj: (b * nq + qi[p], 0)),
        scratch_shapes=[pltpu.VMEM((rows, KCAT), BF16), pltpu.VMEM((rows, LANES), F32),
                        pltpu.VMEM((rows, LANES), F32), pltpu.VMEM((rows, KV_LORA), F32)],
    )
    return pl.pallas_call(
        functools.partial(_attn_prompt_kernel, tq=tq, tk=tk),
        grid_spec=grid_spec,
        out_shape=jax.ShapeDtypeStruct((bsz * t, MLA_HEADS * KV_LORA), BF16),
        compiler_params=_cparams("parallel", "arbitrary"),
        name="attn_prompt",
    )(qi, kj, qcat, kcat)


def _attn_sample_kernel(pt_ref, q_ref, kn_ref, lat_hbm, kr_hbm, o_ref, lat_buf, kr_buf, sems, m_scr, l_scr, acc_scr,
                        *, layer, n_pages, t):
    b = pl.program_id(0)
    nb = pl.num_programs(0)
    slot = b % 2
    rows = MLA_HEADS * t

    def page_copies(bb, sl, i):
        pg = pt_ref[bb, i]
        return (pltpu.make_async_copy(lat_hbm.at[layer, pg], lat_buf.at[sl, i * PAGE_SIZE:(i + 1) * PAGE_SIZE, :],
                                      sems.at[0, sl]),
                pltpu.make_async_copy(kr_hbm.at[layer, pg], kr_buf.at[sl, :, i * PAGE_SIZE:(i + 1) * PAGE_SIZE],
                                      sems.at[1, sl]))

    def start_fetch(bb, sl):
        for i in range(n_pages):
            for cp in page_copies(bb, sl, i):
                cp.start()

    @pl.when(b == 0)
    def _():
        start_fetch(0, 0)

    @pl.when(b + 1 < nb)
    def _():
        start_fetch(b + 1, 1 - slot)

    m_scr[...] = jnp.full((rows, LANES), -jnp.inf, F32)
    l_scr[...] = jnp.zeros((rows, LANES), F32)
    acc_scr[...] = jnp.zeros((rows, KV_LORA), F32)
    qs = jnp.concatenate([q_ref[:, h * KCAT:(h + 1) * KCAT] for h in range(MLA_HEADS)], axis=0).astype(BF16)

    for i in range(n_pages):
        for cp in page_copies(b, slot, i):
            cp.wait()

    klat = lat_buf[slot].astype(BF16)
    krope_t = kr_buf[slot].astype(BF16)
    s = _dot_nt(qs[:, :KV_LORA], klat) + _dot(qs[:, KV_LORA:KV_LORA + QK_ROPE], krope_t)
    _softmax_step(s, klat, m_scr, l_scr, acc_scr)

    kn = jnp.concatenate([kn_ref[...], jnp.zeros((LANES - t, KCAT), F32)], axis=0).astype(BF16)
    sn = _dot_nt(qs, kn)
    qpos = lax.broadcasted_iota(I32, (rows, LANES), 0) & (t - 1)
    kpos = lax.broadcasted_iota(I32, (rows, LANES), 1)
    _softmax_step(jnp.where(kpos <= qpos, sn, -jnp.inf), kn[:, :KV_LORA], m_scr, l_scr, acc_scr)
    inv = _lane_tile(1.0 / l_scr[...], KV_LORA // LANES)
    for h in range(MLA_HEADS):
        o_ref[:, h * KV_LORA:(h + 1) * KV_LORA] = acc_scr[h * t:(h + 1) * t, :] * inv[h * t:(h + 1) * t, :]


def _attn_sample(qcat, kcat, cache_lat, cache_kr, page_table, layer, bsz, t):
    n_pages = page_table.shape[1]
    past = n_pages * PAGE_SIZE
    rows = MLA_HEADS * t
    grid_spec = pltpu.PrefetchScalarGridSpec(
        num_scalar_prefetch=1,
        grid=(bsz,),
        in_specs=[pl.BlockSpec((None, t, MLA_HEADS * KCAT), lambda b, pt: (b, 0, 0)),
                  pl.BlockSpec((None, t, KCAT), lambda b, pt: (b, 0, 0)),
                  pl.BlockSpec(memory_space=pl.ANY), pl.BlockSpec(memory_space=pl.ANY)],
        out_specs=pl.BlockSpec((None, t, MLA_HEADS * KV_LORA), lambda b, pt: (b, 0, 0)),
        scratch_shapes=[pltpu.VMEM((2, past, KV_LORA), F32), pltpu.VMEM((2, QK_ROPE, past), F32),
                        pltpu.SemaphoreType.DMA((2, 2)),
                        pltpu.VMEM((rows, LANES), F32), pltpu.VMEM((rows, LANES), F32),
                        pltpu.VMEM((rows, KV_LORA), F32)],
    )
    out = pl.pallas_call(
        functools.partial(_attn_sample_kernel, layer=layer, n_pages=n_pages, t=t),
        grid_spec=grid_spec,
        out_shape=jax.ShapeDtypeStruct((bsz, t, MLA_HEADS * KV_LORA), F32),
        compiler_params=_cparams("arbitrary"),
        name="attn_sample",
    )(page_table, qcat.reshape(bsz, t, -1), kcat.reshape(bsz, t, -1), cache_lat, jnp.swapaxes(cache_kr, 2, 3))
    return out.reshape(bsz * t, -1)


def _mla_out_kernel(o_ref, wuv_ref, wo_ref, xr_ref, g_ref, b_ref, out_ref):
    o = o_ref[...].astype(BF16)
    parts = [_dot(o[:, h * KV_LORA:(h + 1) * KV_LORA], wuv_ref[h]).astype(BF16) for h in range(MLA_HEADS)]
    h = _dot(jnp.concatenate(parts, axis=1), wo_ref[...])
    out_ref[...] = _layer_norm(DEEPNORM_ALPHA * xr_ref[...] + h, g_ref[...], b_ref[...])


def _mla_out(o, wuv, wo, xres, g, b, tm):
    m = o.shape[0]
    full = lambda a: pl.BlockSpec(a.shape, lambda i: (0,) * a.ndim)
    row = lambda n: pl.BlockSpec((tm, n), lambda i: (i, 0))
    return pl.pallas_call(
        _mla_out_kernel,
        grid=(m // tm,),
        in_specs=[row(MLA_HEADS * KV_LORA), full(wuv), full(wo), row(D_MODEL), full(g), full(b)],
        out_specs=row(D_MODEL),
        out_shape=jax.ShapeDtypeStruct((m, D_MODEL), F32),
        compiler_params=_cparams("parallel"),
        name="mla_out",
    )(o, wuv, wo, xres, g, b)


def _route_kernel(x_ref, rh_ref, rl_ref, bias_ref, pa_ref, pb_ref, wa_ref, wb_ref, ng_ref, *, subs):
    tm = subs * MOE_SUB
    xh, xm = _split2(x_ref[...])
    rh = rh_ref[...]
    logits = _dot_nt(rh, xh) + _dot_nt(rh, xm) + _dot_nt(rl_ref[...], xh)
    aff = jax.nn.sigmoid(logits)
    sel = aff + bias_ref[...]
    sub = lax.broadcasted_iota(I32, (EXPERTS_PER_GROUP, tm), 0)
    for g in range(MOE_GROUPS):
        sl = slice(g * EXPERTS_PER_GROUP, (g + 1) * EXPERTS_PER_GROUP)
        sg, ag = sel[sl, :], aff[sl, :]
        m1 = jnp.max(sg, axis=0, keepdims=True)
        i1 = jnp.min(jnp.where(sg == m1, sub, EXPERTS_PER_GROUP), axis=0, keepdims=True)
        sg2 = jnp.where(sub == i1, -jnp.inf, sg)
        m2 = jnp.max(sg2, axis=0, keepdims=True)
        i2 = jnp.min(jnp.where(sg2 == m2, sub, EXPERTS_PER_GROUP), axis=0, keepdims=True)
        a1 = jnp.sum(jnp.where(sub == i1, ag, 0.0), axis=0, keepdims=True)
        a2 = jnp.sum(jnp.where(sub == i2, ag, 0.0), axis=0, keepdims=True)
        score = m1 + m2
        if g == 0:
            best, ea, eb, va, vb = score, i1, i2, a1, a2
        else:
            better = score > best
            best = jnp.where(better, score, best)
            ea = jnp.where(better, i1 + g * EXPERTS_PER_GROUP, ea)
            eb = jnp.where(better, i2 + g * EXPERTS_PER_GROUP, eb)
            va = jnp.where(better, a1, va)
            vb = jnp.where(better, a2, vb)
    den = va + vb
    wa_ref[...] = va / den
    wb_ref[...] = vb / den

    eid = lax.broadcasted_iota(I32, (N_EXPERTS, tm), 0)
    oh_a = jnp.where(eid == ea, 1.0, 0.0)
    oh_b = jnp.where(eid == eb, 1.0, 0.0)
    before = lax.broadcasted_iota(I32, (MOE_SUB, MOE_SUB), 0) < lax.broadcasted_iota(I32, (MOE_SUB, MOE_SUB), 1)
    upper = jnp.where(before, 1.0, 0.0).astype(BF16)
    lower = jnp.where(lax.broadcasted_iota(I32, (N_EXPERTS, N_EXPERTS), 1)
                      < lax.broadcasted_iota(I32, (N_EXPERTS, N_EXPERTS), 0), 1.0, 0.0).astype(BF16)
    lane = lax.broadcasted_iota(I32, (N_EXPERTS, LANES), 1)
    ng_all = jnp.zeros((N_EXPERTS, LANES), F32)
    pos_a, pos_b = [], []
    for s in range(subs):
        a = oh_a[:, s * MOE_SUB:(s + 1) * MOE_SUB]
        b = oh_b[:, s * MOE_SUB:(s + 1) * MOE_SUB]
        cnt_a = jnp.sum(a, axis=1, keepdims=True)
        cnt = cnt_a + jnp.sum(b, axis=1, keepdims=True)
        ng = jnp.floor((cnt + (MOE_GRAN - 1)) * (1.0 / MOE_GRAN))
        rows = jnp.broadcast_to(ng * MOE_GRAN, (N_EXPERTS, LANES)).astype(BF16)
        start = _dot(lower, rows)[:, :1]
        pos_a.append(jnp.sum(a * (start + _dot(a.astype(BF16), upper)), axis=0, keepdims=True))
        pos_b.append(jnp.sum(b * (start + cnt_a + _dot(b.astype(BF16), upper)), axis=0, keepdims=True))
        ng_all = jnp.where(lane == s, ng, ng_all)
    pa_ref[...] = jnp.concatenate(pos_a, axis=1).astype(I32)
    pb_ref[...] = jnp.concatenate(pos_b, axis=1).astype(I32)
    ng_ref[...] = ng_all


def _route(x, rh, rl, bias, subs):
    m = x.shape[0]
    tm = subs * MOE_SUB
    full = lambda a: pl.BlockSpec(a.shape, lambda i: (0, 0))
    lane_row = pl.BlockSpec((1, tm), lambda i: (0, i))
    row_i = jax.ShapeDtypeStruct((1, m), I32)
    row_f = jax.ShapeDtypeStruct((1, m), F32)
    return pl.pallas_call(
        functools.partial(_route_kernel, subs=subs),
        grid=(m // tm,),
        in_specs=[pl.BlockSpec((tm, D_MODEL), lambda i: (i, 0)), full(rh), full(rl), full(bias)],
        out_specs=[lane_row, lane_row, lane_row, lane_row,
                   pl.BlockSpec((None, N_EXPERTS, LANES), lambda i: (i, 0, 0))],
        out_shape=[row_i, row_i, row_f, row_f, jax.ShapeDtypeStruct((m // tm, N_EXPERTS, LANES), F32)],
        compiler_params=_cparams("parallel"),
        name="moe_route",
    )(x, rh, rl, bias)


def _moe_kernel(ng_ref, gs_ref, nit_ref, x_ref, pa_ref, pb_ref, wa_ref, wb_ref, wg_ref, wu_ref, wd_ref,
                g_ref, b_ref, o_ref, buf, *, ns, subs, eps):
    c = pl.program_id(0)
    st = pl.program_id(1)
    n_sort = ns // subs
    n_exp = N_EXPERTS // eps
    row_id = lax.broadcasted_iota(I32, (MOE_ROWS, MOE_SUB), 0)

    @pl.when(st < n_sort)
    def _():
        for k in range(subs):
            sl = slice(k * MOE_SUB, (k + 1) * MOE_SUB)
            hit = jnp.logical_or(row_id == pa_ref[:, sl], row_id == pb_ref[:, sl])
            perm = jnp.where(hit, 1.0, 0.0).astype(BF16)
            s = st * subs + k
            buf[s, 0:MOE_ROWS, :] = _dot(perm, x_ref[sl, :].astype(BF16)).astype(BF16)
            buf[s, MOE_ROWS:MOE_ROWS + MOE_GRAN, :] = jnp.zeros((MOE_GRAN, D_MODEL), BF16)

    @pl.when(jnp.logical_and(st >= n_sort, st < n_sort + n_exp))
    def _():
        e0 = (st - n_sort) * eps

        def body(w, carry):
            starts, xws = [], []
            for j in range(eps):
                row0s = []
                for s in range(ns):
                    idx = (c * ns + s) * N_EXPERTS + e0 + j
                    row0 = jnp.where(w < ng_ref[idx], gs_ref[idx] + w * MOE_GRAN, MOE_ROWS)
                    row0s.append(pl.multiple_of(row0, MOE_GRAN))
                starts.append(row0s)
                xws.append(jnp.concatenate([buf[s, pl.ds(row0s[s], MOE_GRAN), :] for s in range(ns)], axis=0))
            ys = []
            for j in range(eps):
                h = _silu(_dot(xws[j], wg_ref[j])) * _dot(xws[j], wu_ref[j])
                ys.append(_dot(h.astype(BF16), wd_ref[j]).astype(BF16))
            for j in range(eps):
                for s in range(ns):
                    buf[s, pl.ds(starts[j][s], MOE_GRAN), :] = ys[j][s * MOE_GRAN:(s + 1) * MOE_GRAN, :]
            return carry

        trips = nit_ref[c * N_EXPERTS + e0]
        for j in range(1, eps):
            trips = jnp.maximum(trips, nit_ref[c * N_EXPERTS + e0 + j])
        lax.fori_loop(0, trips, body, 0)

    @pl.when(st >= n_sort + n_exp)
    def _():
        for k in range(subs):
            sl = slice(k * MOE_SUB, (k + 1) * MOE_SUB)
            gate = (jnp.where(row_id == pa_ref[:, sl], wa_ref[:, sl], 0.0)
                    + jnp.where(row_id == pb_ref[:, sl], wb_ref[:, sl], 0.0)).astype(BF16)
            s = (st - n_sort - n_exp) * subs + k
            y = _dot_tn(gate, buf[s, 0:MOE_ROWS, :])
            o_ref[sl, :] = _layer_norm(DEEPNORM_ALPHA * x_ref[sl, :] + y, g_ref[...], b_ref[...])


def _hier_moe_ln(x, prm, g, b, ns):
    m = x.shape[0]
    subs, eps = MOE_SUBS_PER_STEP, MOE_EXPERTS_PER_STEP
    tm = subs * MOE_SUB
    chunk = ns * MOE_SUB
    n_chunks = m // chunk
    n_sort = ns // subs
    n_exp = N_EXPERTS // eps
    pa, pb, wa, wb, ng = _route(x, prm["router_hi"], prm["router_lo"], prm["router_bias"], subs)
    ng = jnp.transpose(ng[:, :, :subs].astype(I32), (0, 2, 1)).reshape(n_chunks, ns, N_EXPERTS)
    gstart = (jnp.cumsum(ng, axis=-1) - ng) * MOE_GRAN
    nit = jnp.max(ng, axis=1)

    def tok_idx(c, st, *_):
        phase = jnp.where(st < n_sort, st, jnp.where(st >= n_sort + n_exp, st - n_sort - n_exp, n_sort - 1))
        return c * n_sort + phase

    def exp_idx(c, st, *_):
        return (prm["layer"], jnp.clip(st - n_sort, 0, n_exp - 1), 0, 0)

    lane_row = pl.BlockSpec((1, tm), lambda c, st, *_: (0, tok_idx(c, st)))
    full = lambda a: pl.BlockSpec(a.shape, lambda c, st, *_: (0, 0))
    grid_spec = pltpu.PrefetchScalarGridSpec(
        num_scalar_prefetch=3,
        grid=(n_chunks, 2 * n_sort + n_exp),
        in_specs=[pl.BlockSpec((tm, D_MODEL), lambda c, st, *_: (tok_idx(c, st), 0)),
                  lane_row, lane_row, lane_row, lane_row,
                  pl.BlockSpec((None, eps, D_MODEL, EXPERT_FF), exp_idx),
                  pl.BlockSpec((None, eps, D_MODEL, EXPERT_FF), exp_idx),
                  pl.BlockSpec((None, eps, EXPERT_FF, D_MODEL), exp_idx),
                  full(g), full(b)],
        out_specs=pl.BlockSpec((tm, D_MODEL),
                               lambda c, st, *_: (c * n_sort + jnp.clip(st - n_sort - n_exp, 0, n_sort - 1), 0)),
        scratch_shapes=[pltpu.VMEM((ns, MOE_ROWS + MOE_GRAN, D_MODEL), BF16)],
    )
    return pl.pallas_call(
        functools.partial(_moe_kernel, ns=ns, subs=subs, eps=eps),
        grid_spec=grid_spec,
        out_shape=jax.ShapeDtypeStruct((m, D_MODEL), F32),
        compiler_params=_cparams("arbitrary", "arbitrary"),
        name="moe_experts",
    )(ng.reshape(-1), gstart.reshape(-1), nit.reshape(-1), x, pa, pb, wa, wb,
      prm["w_gate"], prm["w_up"], prm["w_down"], g, b)


def _pad_cols(a, n):
    return jnp.pad(a, ((0, 0), (0, n - a.shape[1])))


def _ssd_params(w_in, conv_w, conv_b, dt_bias, a_log, d_skip, norm_w, w_out):
    wdt = _pad_cols(w_in[:, D_INNER + CONV_DIM:], LANES)
    wdh = wdt.astype(BF16)
    head_of_col = jnp.arange(D_INNER, dtype=I32) // SSD_HEAD_DIM
    expand = (jnp.arange(LANES, dtype=I32)[:, None] == head_of_col[None, :]).astype(BF16)
    return {
        "wz": w_in[:, :D_INNER].astype(BF16),
        "wx": w_in[:, D_INNER:D_INNER + CONV_DIM].astype(BF16),
        "wdh": wdh,
        "wdl": (wdt - wdh.astype(F32)).astype(BF16),
        "conv_w": conv_w,
        "conv_b": conv_b[None, :],
        "dt_bias": _pad_cols(dt_bias[None, :], LANES),
        "a_log": _pad_cols(a_log[None, :], LANES),
        "d_skip": jnp.repeat(d_skip, SSD_HEAD_DIM)[None, :],
        "norm_w": norm_w[None, :],
        "expand": expand,
        "w_out": w_out.astype(BF16),
    }


def _mla_params(w_dqkv, q_norm, kv_norm, w_uq, w_uk, w_uv, w_o):
    wr = jnp.pad(w_uq[:, :, QK_NOPE:], ((0, 0), (0, 0), (0, LANES - QK_ROPE)))
    return {
        "w_down": _pad_cols(w_dqkv, DCOLS).astype(BF16),
        "q_norm": q_norm[None, :],
        "kv_norm": kv_norm[None, :],
        "w_qn": w_uq[:, :, :QK_NOPE].reshape(Q_LORA, MLA_HEADS * QK_NOPE).astype(BF16),
        "w_qr": wr.reshape(Q_LORA, MLA_HEADS * LANES).astype(BF16),
        "w_uk_t": jnp.transpose(w_uk, (1, 2, 0)).astype(BF16),
        "w_uv": jnp.transpose(w_uv, (1, 0, 2)).astype(BF16),
        "w_o": w_o.astype(BF16),
    }


def _moe_params(layer, router, router_bias, w_gate, w_up, w_down):
    rt = router.T
    rh = rt.astype(BF16)
    return {
        "layer": layer,
        "router_hi": rh,
        "router_lo": (rt - rh.astype(F32)).astype(BF16),
        "router_bias": router_bias[:, None],
        "w_gate": w_gate,
        "w_up": w_up,
        "w_down": w_down,
    }


def _rope_tables(pos):
    half = QK_ROPE // 2
    inv_freq = ROPE_THETA ** (-jnp.arange(half, dtype=F32) / half)
    ang = pos.astype(F32)[:, None] * inv_freq[None, :]
    cos, sin = jnp.cos(ang), jnp.sin(ang)
    zeros = jnp.zeros((pos.shape[0], LANES - QK_ROPE), F32)
    return jnp.concatenate([cos, cos, zeros], axis=1), jnp.concatenate([-sin, sin, zeros], axis=1)


def _tile(m, want):
    return min(m, want)


def kernel(x_prompt, x_sample, cache_mla_latent, cache_mla_krope, state_ssd, state_conv, page_table,
           ln_mix_g, ln_mix_b, ln_ffn_g, ln_ffn_b,
           ssd_w_in, ssd_conv_w, ssd_conv_b, ssd_dt_bias, ssd_a_log, ssd_d_skip, ssd_norm_w, ssd_w_out,
           mla_w_dqkv, mla_q_norm, mla_kv_norm, mla_w_uq, mla_w_uk, mla_w_uv, mla_w_o,
           moe_router, moe_router_bias, moe_w_gate, moe_w_up, moe_w_down):
    bp, tp, _ = x_prompt.shape
    bs, ts, _ = x_sample.shape
    past_len = page_table.shape[1] * PAGE_SIZE
    xp = x_prompt.reshape(bp * tp, D_MODEL)
    xs = x_sample.reshape(bs * ts, D_MODEL)
    lp = min(SSD_CHUNK, tp)

    cos_p, sin_p = _rope_tables(jnp.arange(tp))
    cos_s, sin_s = _rope_tables(past_len + jnp.arange(ts))
    cos_s = jnp.tile(cos_s, (bs, 1))
    sin_s = jnp.tile(sin_s, (bs, 1))

    w_gate_b, w_up_b, w_down_b = moe_w_gate.astype(BF16), moe_w_up.astype(BF16), moe_w_down.astype(BF16)
    p_lat, p_kr, p_ssm, p_conv = [], [], [], []
    s_lat, s_kr, s_ssm, s_conv = [], [], [], []
    for i in range(DEPTH):
        j = i // 2
        g_mix, b_mix = ln_mix_g[i][None, :], ln_mix_b[i][None, :]
        if i % 2 == 0:
            prm = _ssd_params(ssd_w_in[j], ssd_conv_w[j], ssd_conv_b[j], ssd_dt_bias[j], ssd_a_log[j],
                              ssd_d_skip[j], ssd_norm_w[j], ssd_w_out[j])
            z, xbc, dt = _ssd_inproj(xp, prm["wz"], prm["wx"], prm["wdh"], prm["wdl"], BF16, _tile(xp.shape[0], 256))
            y, cp, sp = _ssd_core(xbc, z, dt, prm, bp, tp, lp, BF16)
            xp = _proj_ln(y, prm["w_out"], xp, g_mix, b_mix, _tile(xp.shape[0], 512))
            z, xbc, dt = _ssd_inproj(xs, prm["wz"], prm["wx"], prm["wdh"], prm["wdl"], F32, _tile(xs.shape[0], 256))
            y, cs, ss = _ssd_core(xbc, z, dt, prm, bs, ts, ts, F32, init=(state_conv, state_ssd, j))
            xs = _proj_ln(y, prm["w_out"], xs, g_mix, b_mix, _tile(xs.shape[0], 512))
            p_conv.append(cp)
            p_ssm.append(sp)
            s_conv.append(cs)
            s_ssm.append(ss)
        else:
            prm = _mla_params(mla_w_dqkv[j], mla_q_norm[j], mla_kv_norm[j], mla_w_uq[j], mla_w_uk[j],
                              mla_w_uv[j], mla_w_o[j])
            tm = _tile(tp, 512)
            cq, lat, kr, kcat = _mla_down(xp, prm["w_down"], prm["q_norm"], prm["kv_norm"], cos_p, sin_p, BF16, tm)
            qcat = _mla_q(cq, prm["w_qn"], prm["w_qr"], prm["w_uk_t"], cos_p, sin_p, BF16, tm)
            o = _attn_prompt(qcat, kcat, bp, tp, tm, tm)
            xp = _mla_out(o, prm["w_uv"], prm["w_o"], xp, g_mix, b_mix, tm)
            p_lat.append(lat.reshape(bp, tp, KV_LORA))
            p_kr.append(kr.reshape(bp, tp, QK_ROPE))
            tm = _tile(xs.shape[0], 512)
            cq, lat, kr, kcat = _mla_down(xs, prm["w_down"], prm["q_norm"], prm["kv_norm"], cos_s, sin_s, F32, tm)
            qcat = _mla_q(cq, prm["w_qn"], prm["w_qr"], prm["w_uk_t"], cos_s, sin_s, F32, tm)
            o = _attn_sample(qcat, kcat, cache_mla_latent, cache_mla_krope, page_table, j, bs, ts)
            xs = _mla_out(o, prm["w_uv"], prm["w_o"], xs, g_mix, b_mix, tm)
            s_lat.append(lat.reshape(bs, ts, KV_LORA))
            s_kr.append(kr.reshape(bs, ts, QK_ROPE))
        mprm = _moe_params(i, moe_router[i], moe_router_bias[i], w_gate_b, w_up_b, w_down_b)
        g_ffn, b_ffn = ln_ffn_g[i][None, :], ln_ffn_b[i][None, :]
        xp = _hier_moe_ln(xp, mprm, g_ffn, b_ffn, min(8, xp.shape[0] // MOE_SUB))
        xs = _hier_moe_ln(xs, mprm, g_ffn, b_ffn, min(8, xs.shape[0] // MOE_SUB))

    return (xp.reshape(bp, tp, D_MODEL), xs.reshape(bs, ts, D_MODEL),
            jnp.stack(p_lat), jnp.stack(p_kr), jnp.stack(p_ssm), jnp.stack(p_conv),
            jnp.stack(s_lat), jnp.stack(s_kr), jnp.stack(s_ssm), jnp.stack(s_conv))
```

```python
import functools

import jax
import jax.numpy as jnp
from jax import lax
from jax.experimental import pallas as pl
from jax.experimental.pallas import tpu as pltpu

F32 = jnp.float32
BF16 = jnp.bfloat16
I32 = jnp.int32

D_MODEL = 1024
DEPTH = 4
PAGE_SIZE = 128
D_INNER = 2048
SSD_HEAD_DIM = 64
SSD_HEADS = 32
SSD_GROUPS = 4
SSD_STATE = 128
SSD_CONV = 4
SSD_CHUNK = 128
CONV_DIM = D_INNER + 2 * SSD_GROUPS * SSD_STATE
GROUP_INNER = D_INNER // SSD_GROUPS
MLA_HEADS = 8
Q_LORA = 384
KV_LORA = 256
QK_NOPE = 128
QK_ROPE = 64
V_HEAD = 128
ROPE_THETA = 10000.0
MOE_GROUPS = 4
EXPERTS_PER_GROUP = 8
N_EXPERTS = 32
EXPERT_FF = 256
DEEPNORM_ALPHA = (2.0 * DEPTH) ** 0.25
LN_EPS = 1e-5
RMS_EPS = 1e-6
ATTN_SCALE = (QK_NOPE + QK_ROPE) ** -0.5

LANES = 128
KCAT = KV_LORA + LANES
DCOLS = Q_LORA + KV_LORA + LANES
MOE_SUB = 256
MOE_GRAN = 16
MOE_ROWS = 2 * MOE_SUB + N_EXPERTS * MOE_GRAN
MOE_SUBS_PER_STEP = 2
MOE_EXPERTS_PER_STEP = 4
VMEM_LIMIT = 56 * 1024 * 1024


def _cparams(*sem):
    return pltpu.CompilerParams(dimension_semantics=sem, vmem_limit_bytes=VMEM_LIMIT)


def _dot(a, b):
    return jnp.dot(a, b, preferred_element_type=F32)


def _dot_nt(a, b):
    return lax.dot_general(a, b, (((1,), (1,)), ((), ())), preferred_element_type=F32)


def _dot_tn(a, b):
    return lax.dot_general(a, b, (((0,), (0,)), ((), ())), preferred_element_type=F32)


def _split2(x):
    hi = x.astype(BF16)
    return hi, (x - hi.astype(F32)).astype(BF16)


def _split3(x):
    hi = x.astype(BF16)
    r = x - hi.astype(F32)
    mid = r.astype(BF16)
    return hi, mid, (r - mid.astype(F32)).astype(BF16)


def _silu(x):
    return x * jax.nn.sigmoid(x)


def _softplus(x):
    return jnp.maximum(x, 0.0) + jnp.log(1.0 + jnp.exp(-jnp.abs(x)))


def _layer_norm(v, g, b):
    mu = jnp.mean(v, axis=-1, keepdims=True)
    d = v - mu
    var = jnp.mean(d * d, axis=-1, keepdims=True)
    return d * lax.rsqrt(var + LN_EPS) * g + b


def _rms_norm(v, g):
    return v * lax.rsqrt(jnp.mean(v * v, axis=-1, keepdims=True) + RMS_EPS) * g


def _rope128(x, cos, sin):
    lane = lax.broadcasted_iota(I32, x.shape, 1)
    partner = jnp.where(lane < QK_ROPE // 2, pltpu.roll(x, LANES - QK_ROPE // 2, 1), pltpu.roll(x, QK_ROPE // 2, 1))
    return x * cos + partner * sin


def _causal_conv(xraw, xp_scr, cw_ref, cb_ref, n):
    xp_scr[8:8 + n, :] = xraw
    xp = xp_scr[...]
    acc = cb_ref[...]
    for k in range(SSD_CONV - 1):
        acc = acc + pltpu.roll(xp, SSD_CONV - 1 - k, 0)[8:8 + n, :] * cw_ref[k:k + 1, :]
    acc = acc + xraw * cw_ref[SSD_CONV - 1:SSD_CONV, :]
    return _silu(acc)


def _ssd_inproj_kernel(*refs, conv, tiles_per_seq):
    if conv:
        (x_ref, wz_ref, wx_ref, wdh_ref, wdl_ref, dtb_ref, cw_ref, cb_ref,
         zs_ref, dt_ref, xs_ref, bc_ref, tail_ref, xp_scr) = refs
    else:
        x_ref, wz_ref, wx_ref, wdh_ref, wdl_ref, dtb_ref, zs_ref, dt_ref, xbc_ref = refs
    if conv:
        @pl.when(pl.program_id(0) % tiles_per_seq == 0)
        def _():
            xp_scr[0:8, :] = jnp.zeros((8, CONV_DIM), F32)

    xh, xm = _split2(x_ref[...])
    xbc = _dot(xh, wx_ref[...])
    if conv:
        tm = xbc.shape[0]
        act = _causal_conv(xbc, xp_scr, cw_ref, cb_ref, tm)
        tail = xp_scr[tm:tm + 8, :]
        tail_ref[...] = tail
        xp_scr[0:8, :] = tail
        xs_ref[...] = act[:, :D_INNER]
        bc_ref[...] = act[:, D_INNER:].astype(BF16)
    else:
        xbc_ref[...] = xbc
    zs_ref[...] = _silu(_dot(xh, wz_ref[...])).astype(zs_ref.dtype)
    wdh = wdh_ref[...]
    dt_ref[...] = _softplus(_dot(xh, wdh) + _dot(xm, wdh) + _dot(xh, wdl_ref[...]) + dtb_ref[...])


def _ssd_inproj(x, prm, z_dtype, tm, seq_len=None):
    m = x.shape[0]
    conv = seq_len is not None
    full = lambda a: pl.BlockSpec(a.shape, lambda i: (0, 0))
    row = lambda n: pl.BlockSpec((tm, n), lambda i: (i, 0))
    args = [x, prm["wz"], prm["wx"], prm["wdh"], prm["wdl"], prm["dt_bias"]]
    out_specs = [row(D_INNER), row(LANES)]
    out_shape = [jax.ShapeDtypeStruct((m, D_INNER), z_dtype), jax.ShapeDtypeStruct((m, LANES), F32)]
    scratch = []
    if conv:
        args += [prm["conv_w"], prm["conv_b"]]
        out_specs += [row(D_INNER), row(CONV_DIM - D_INNER), pl.BlockSpec((None, 8, CONV_DIM), lambda i: (i, 0, 0))]
        out_shape += [jax.ShapeDtypeStruct((m, D_INNER), F32), jax.ShapeDtypeStruct((m, CONV_DIM - D_INNER), BF16),
                      jax.ShapeDtypeStruct((m // tm, 8, CONV_DIM), F32)]
        scratch = [pltpu.VMEM((tm + 8, CONV_DIM), F32)]
    else:
        out_specs += [row(CONV_DIM)]
        out_shape += [jax.ShapeDtypeStruct((m, CONV_DIM), F32)]
    return pl.pallas_call(
        functools.partial(_ssd_inproj_kernel, conv=conv, tiles_per_seq=(seq_len // tm if conv else 1)),
        grid=(m // tm,),
        in_specs=[row(D_MODEL)] + [full(a) for a in args[1:]],
        out_specs=out_specs,
        out_shape=out_shape,
        scratch_shapes=scratch,
        compiler_params=_cparams("arbitrary"),
        name="ssd_inproj_conv" if conv else "ssd_inproj",
    )(*args)


def _ssd_kernel(*refs, L, has_init):
    if has_init:
        (xbc_ref, zs_ref, dt_ref, cw_ref, cb_ref, alog_ref, dskip_ref, nw_ref, e_ref,
         conv0_ref, ssm0_ref, y_ref, convo_ref, sso_ref, xp_scr, h_scr) = refs
    else:
        (xs_ref, bc_ref, zs_ref, dt_ref, alog_ref, dskip_ref, nw_ref, e_ref, y_ref, sso_ref, h_scr) = refs
    c = pl.program_id(1)
    nc = pl.num_programs(1)
    LK = max(L, LANES)
    K1 = SSD_CONV - 1
    HPG = SSD_HEADS // SSD_GROUPS
    NBC = SSD_GROUPS * SSD_STATE

    @pl.when(c == 0)
    def _():
        if has_init:
            xp_scr[8 - K1:8, :] = conv0_ref[...]
            h_scr[...] = ssm0_ref[...].reshape(D_INNER, SSD_STATE)
        else:
            h_scr[...] = jnp.zeros((D_INNER, SSD_STATE), F32)

    def pad_rows(a):
        if L == LK:
            return a
        return jnp.concatenate([a, jnp.zeros((LK - L, a.shape[1]), a.dtype)], axis=0)

    if has_init:
        act = _causal_conv(xbc_ref[...], xp_scr, cw_ref, cb_ref, L)
        tail = xp_scr[L + 8 - K1:L + 8, :]

        @pl.when(c == nc - 1)
        def _():
            convo_ref[...] = tail

        xp_scr[8 - K1:8, :] = tail
        xs = act[:, :D_INNER]
        bm = act[:, D_INNER:D_INNER + NBC].astype(BF16)
        cm = act[:, D_INNER + NBC:].astype(BF16)
    else:
        xs = xs_ref[...]
        bm = bc_ref[:, :NBC]
        cm = bc_ref[:, NBC:]

    dtv = dt_ref[...]
    d_a = dtv * (-jnp.exp(alog_ref[...]))

    row = lax.broadcasted_iota(I32, (L, LK), 0)
    col = lax.broadcasted_iota(I32, (L, LK), 1)
    causal = col <= row
    tril = jnp.where(causal, 1.0, 0.0).astype(BF16)
    cumc = sum(_dot(tril, p) for p in _split3(pad_rows(d_a)))
    ir = lax.broadcasted_iota(I32, (LANES, LANES), 0)
    ic = lax.broadcasted_iota(I32, (LANES, LANES), 1)
    ident = jnp.where(ir == ic, 1.0, 0.0).astype(BF16)
    cumr = sum(_dot_nt(ident, p) for p in _split3(pad_rows(cumc)))

    ecum = jnp.exp(cumc)
    last = cumc[L - 1:L, :]
    ws = jnp.exp(last - cumc)
    elc = jnp.exp(cumr[:, L - 1:L])
    e_mat = e_ref[...]
    dt_x = _dot(dtv.astype(BF16), e_mat)
    ecum_x = _dot(ecum.astype(BF16), e_mat)
    ws_x = _dot(ws.astype(BF16), e_mat)

    xdt = xs * dt_x
    xdt_b = xdt.astype(BF16)
    xw_p = pad_rows((xdt * ws_x).astype(BF16))
    lo_half = lax.broadcasted_iota(I32, (LK, LANES), 1) < SSD_HEAD_DIM
    zero_b = jnp.zeros((LK, LANES), BF16)
    dskip = dskip_ref[...]
    neg_inf = jnp.float32(-jnp.inf)

    ys = []
    for g in range(SSD_GROUPS):
        bc_g = bm[:, g * SSD_STATE:(g + 1) * SSD_STATE]
        cc_g = cm[:, g * SSD_STATE:(g + 1) * SSD_STATE]
        bc_p = pad_rows(bc_g)
        cb = _dot_nt(cc_g, bc_p)
        h_g = h_scr[g * GROUP_INNER:(g + 1) * GROUP_INNER, :]
        ys_g = _dot_nt(cc_g, h_g.astype(BF16))
        for q in range(HPG // 2):
            ha = g * HPG + 2 * q
            c0 = (g * (HPG // 2) + q) * LANES
            dec = []
            for hh in (ha, ha + 1):
                seg = cumc[:, hh:hh + 1] - cumr[hh:hh + 1, :]
                dec.append(cb * jnp.exp(jnp.where(causal, seg, neg_inf)))
            m_cat = jnp.concatenate(dec, axis=1).astype(BF16)
            x_p = pad_rows(xdt_b[:, c0:c0 + LANES])
            xbd = jnp.concatenate([jnp.where(lo_half, x_p, zero_b), jnp.where(lo_half, zero_b, x_p)], axis=0)
            y_p = _dot(m_cat, xbd)
            y_p = y_p + ys_g[:, q * LANES:(q + 1) * LANES] * ecum_x[:, c0:c0 + LANES]
            y_p = y_p + dskip[:, c0:c0 + LANES] * xs[:, c0:c0 + LANES]
            ys.append(y_p)
        upd = _dot_tn(xw_p[:, g * GROUP_INNER:(g + 1) * GROUP_INNER], bc_p)
        scaled = [h_g[j * SSD_HEAD_DIM:(j + 1) * SSD_HEAD_DIM, :] * elc[g * HPG + j:g * HPG + j + 1, :]
                  for j in range(HPG)]
        h_scr[g * GROUP_INNER:(g + 1) * GROUP_INNER, :] = jnp.concatenate(scaled, axis=0) + upd

    y = jnp.concatenate(ys, axis=1)
    gated = y * zs_ref[...].astype(F32)
    outs = []
    for g in range(SSD_GROUPS):
        gg = gated[:, g * GROUP_INNER:(g + 1) * GROUP_INNER]
        outs.append(gg * lax.rsqrt(jnp.mean(gg * gg, axis=-1, keepdims=True) + RMS_EPS))
    y_ref[...] = (jnp.concatenate(outs, axis=1) * nw_ref[...]).astype(y_ref.dtype)

    @pl.when(c == nc - 1)
    def _():
        sso_ref[...] = h_scr[...].reshape(SSD_HEADS, SSD_HEAD_DIM, SSD_STATE)


def _ssd_core(seq_in, zs, dt, prm, bsz, t, L, y_dtype, init=None):
    nc = t // L
    m = bsz * t
    has_init = init is not None
    tok = lambda n: pl.BlockSpec((L, n), lambda b, c: (b * nc + c, 0))
    full = lambda a: pl.BlockSpec(a.shape, lambda b, c: (0, 0))
    params = [prm["a_log"], prm["d_skip"], prm["norm_w"], prm["expand"]]
    if has_init:
        params = [prm["conv_w"], prm["conv_b"]] + params
    in_specs = [tok(a.shape[1]) for a in seq_in] + [tok(D_INNER), tok(LANES)] + [full(a) for a in params]
    args = list(seq_in) + [zs, dt] + params
    out_specs = [tok(D_INNER)]
    out_shape = [jax.ShapeDtypeStruct((m, D_INNER), y_dtype)]
    scratch = []
    if has_init:
        conv0, ssm0, layer = init
        in_specs += [pl.BlockSpec((None, None, SSD_CONV - 1, CONV_DIM), lambda b, c: (layer, b, 0, 0)),
                     pl.BlockSpec((None, None, SSD_HEADS, SSD_HEAD_DIM, SSD_STATE), lambda b, c: (layer, b, 0, 0, 0))]
        args += [conv0, ssm0]
        out_specs += [pl.BlockSpec((None, SSD_CONV - 1, CONV_DIM), lambda b, c: (b, 0, 0))]
        out_shape += [jax.ShapeDtypeStruct((bsz, SSD_CONV - 1, CONV_DIM), F32)]
        scratch = [pltpu.VMEM((L + 8, CONV_DIM), F32)]
    out_specs += [pl.BlockSpec((None, SSD_HEADS, SSD_HEAD_DIM, SSD_STATE), lambda b, c: (b, 0, 0, 0))]
    out_shape += [jax.ShapeDtypeStruct((bsz, SSD_HEADS, SSD_HEAD_DIM, SSD_STATE), F32)]
    return pl.pallas_call(
        functools.partial(_ssd_kernel, L=L, has_init=has_init),
        grid=(bsz, nc),
        in_specs=in_specs,
        out_specs=out_specs,
        out_shape=out_shape,
        scratch_shapes=scratch + [pltpu.VMEM((D_INNER, SSD_STATE), F32)],
        compiler_params=_cparams("parallel", "arbitrary"),
        name="ssd_core_init" if has_init else "ssd_core",
    )(*args)


def _proj_ln_kernel(y_ref, w_ref, xr_ref, g_ref, b_ref, o_ref):
    h = _dot(y_ref[...].astype(BF16), w_ref[...])
    o_ref[...] = _layer_norm(DEEPNORM_ALPHA * xr_ref[...] + h, g_ref[...], b_ref[...])


def _proj_ln(y, w, xres, g, b, tm):
    m, k = y.shape
    full = lambda a: pl.BlockSpec(a.shape, lambda i: (0, 0))
    row = lambda n: pl.BlockSpec((tm, n), lambda i: (i, 0))
    return pl.pallas_call(
        _proj_ln_kernel,
        grid=(m // tm,),
        in_specs=[row(k), full(w), row(D_MODEL), full(g), full(b)],
        out_specs=row(D_MODEL),
        out_shape=jax.ShapeDtypeStruct((m, D_MODEL), F32),
        compiler_params=_cparams("parallel"),
        name="proj_ln",
    )(y, w, xres, g, b)


def _mla_down_kernel(x_ref, w_ref, qn_ref, kvn_ref, cos_ref, sin_ref, cq_ref, ckv_ref, kr_ref, kcat_ref):
    c = _dot(x_ref[...].astype(BF16), w_ref[...])
    cq = _rms_norm(c[:, :Q_LORA], qn_ref[...])
    ckv = _rms_norm(c[:, Q_LORA:Q_LORA + KV_LORA], kvn_ref[...])
    kr = _rope128(c[:, Q_LORA + KV_LORA:], cos_ref[...], sin_ref[...])
    cq_ref[...] = cq.astype(BF16)
    ckv_ref[...] = ckv
    kr_ref[...] = kr[:, :QK_ROPE]
    kcat_ref[...] = jnp.concatenate([ckv, kr], axis=1).astype(kcat_ref.dtype)


def _mla_down(x, w, qn, kvn, cos, sin, kcat_dtype, tm):
    m = x.shape[0]
    nper = cos.shape[0] // tm
    full = lambda a: pl.BlockSpec(a.shape, lambda i: (0, 0))
    row = lambda n: pl.BlockSpec((tm, n), lambda i: (i, 0))
    tab = pl.BlockSpec((tm, LANES), lambda i: (i % nper, 0))
    return pl.pallas_call(
        _mla_down_kernel,
        grid=(m // tm,),
        in_specs=[row(D_MODEL), full(w), full(qn), full(kvn), tab, tab],
        out_specs=[row(Q_LORA), row(KV_LORA), row(QK_ROPE), row(KCAT)],
        out_shape=[jax.ShapeDtypeStruct((m, Q_LORA), BF16),
                   jax.ShapeDtypeStruct((m, KV_LORA), F32),
                   jax.ShapeDtypeStruct((m, QK_ROPE), F32),
                   jax.ShapeDtypeStruct((m, KCAT), kcat_dtype)],
        compiler_params=_cparams("parallel"),
        name="mla_down",
    )(x, w, qn, kvn, cos, sin)


def _mla_q_kernel(cq_ref, wn_ref, wr_ref, wuk_ref, cos_ref, sin_ref, q_ref):
    cq = cq_ref[...]
    qn = _dot(cq, wn_ref[...]).astype(BF16)
    qr = _dot(cq, wr_ref[...])
    cos = cos_ref[...]
    sin = sin_ref[...]
    for h in range(MLA_HEADS):
        ql = _dot(qn[:, h * QK_NOPE:(h + 1) * QK_NOPE], wuk_ref[h]) * ATTN_SCALE
        rp = _rope128(qr[:, h * LANES:(h + 1) * LANES], cos, sin) * ATTN_SCALE
        q_ref[:, h * KCAT:h * KCAT + KV_LORA] = ql.astype(q_ref.dtype)
        q_ref[:, h * KCAT + KV_LORA:(h + 1) * KCAT] = rp.astype(q_ref.dtype)


def _mla_q(cq, wn, wr, wuk, cos, sin, q_dtype, tm):
    m = cq.shape[0]
    nper = cos.shape[0] // tm
    full2 = lambda a: pl.BlockSpec(a.shape, lambda i: (0, 0))
    tab = pl.BlockSpec((tm, LANES), lambda i: (i % nper, 0))
    return pl.pallas_call(
        _mla_q_kernel,
        grid=(m // tm,),
        in_specs=[pl.BlockSpec((tm, Q_LORA), lambda i: (i, 0)), full2(wn), full2(wr),
                  pl.BlockSpec(wuk.shape, lambda i: (0, 0, 0)), tab, tab],
        out_specs=pl.BlockSpec((tm, MLA_HEADS * KCAT), lambda i: (i, 0)),
        out_shape=jax.ShapeDtypeStruct((m, MLA_HEADS * KCAT), q_dtype),
        compiler_params=_cparams("parallel"),
        name="mla_q",
    )(cq, wn, wr, wuk, cos, sin)


def _lane_tile(a, n):
    return a if n == 1 else jnp.concatenate([a] * n, axis=1)


def _softmax_step(s, v, m_scr, l_scr, acc_scr):
    m_prev = m_scr[...]
    m_new = jnp.maximum(m_prev, jnp.max(s, axis=-1, keepdims=True))
    corr = jnp.exp(m_prev - m_new)
    p = jnp.exp(s - _lane_tile(m_new, s.shape[1] // LANES))
    l_scr[...] = corr * l_scr[...] + jnp.sum(p, axis=-1, keepdims=True)
    acc_scr[...] = _lane_tile(corr, KV_LORA // LANES) * acc_scr[...] + _dot(p.astype(BF16), v)
    m_scr[...] = m_new


def _attn_prompt_kernel(qi_ref, kj_ref, q_ref, k_ref, o_ref, qs_scr, m_scr, l_scr, acc_scr, *, tq, tk):
    p = pl.program_id(1)
    i = qi_ref[p]
    j = kj_ref[p]
    rows = MLA_HEADS * tq

    @pl.when(j == 0)
    def _():
        for h in range(MLA_HEADS):
            qs_scr[h * tq:(h + 1) * tq, :] = q_ref[:, h * KCAT:(h + 1) * KCAT]
        m_scr[...] = jnp.full((rows, LANES), -jnp.inf, F32)
        l_scr[...] = jnp.zeros((rows, LANES), F32)
        acc_scr[...] = jnp.zeros((rows, KV_LORA), F32)

    k = k_ref[...]
    s = _dot_nt(qs_scr[...], k)
    v = k[:, :KV_LORA]
    crosses = (j + 1) * tk - 1 > i * tq

    @pl.when(crosses)
    def _():
        qpos = (lax.broadcasted_iota(I32, (rows, tk), 0) & (tq - 1)) + i * tq
        kpos = lax.broadcasted_iota(I32, (rows, tk), 1) + j * tk
        _softmax_step(jnp.where(kpos <= qpos, s, -jnp.inf), v, m_scr, l_scr, acc_scr)

    @pl.when(jnp.logical_not(crosses))
    def _():
        _softmax_step(s, v, m_scr, l_scr, acc_scr)

    @pl.when(j == ((i + 1) * tq - 1) // tk)
    def _():
        inv = _lane_tile(1.0 / l_scr[...], KV_LORA // LANES)
        for h in range(MLA_HEADS):
            o_ref[:, h * KV_LORA:(h + 1) * KV_LORA] = (
                acc_scr[h * tq:(h + 1) * tq, :] * inv[h * tq:(h + 1) * tq, :]).astype(o_ref.dtype)


def _attn_prompt(qcat, kcat, bsz, t, tq, tk):
    nq, nk = t // tq, t // tk
    pairs = [(i, j) for i in range(nq) for j in range(((i + 1) * tq - 1) // tk + 1)]
    qi = jnp.asarray([p[0] for p in pairs], I32)
    kj = jnp.asarray([p[1] for p in pairs], I32)
    rows = MLA_HEADS * tq
    grid_spec = pltpu.PrefetchScalarGridSpec(
        num_scalar_prefetch=2,
        grid=(bsz, len(pairs)),
        in_specs=[pl.BlockSpec((tq, MLA_HEADS * KCAT), lambda b, p, qi, kj: (b * nq + qi[p], 0)),
                  pl.BlockSpec((tk, KCAT), lambda b, p, qi, kj: (b * nk + kj[p], 0))],
        out_specs=pl.BlockSpec((tq, MLA_HEADS * KV_LORA), lambda b, p, qi, kj: (b * nq + qi[p], 0)),
        scratch_shapes=[pltpu.VMEM((rows, KCAT), BF16), pltpu.VMEM((rows, LANES), F32),
                        pltpu.VMEM((rows, LANES), F32), pltpu.VMEM((rows, KV_LORA), F32)],
    )
    return pl.pallas_call(
        functools.partial(_attn_prompt_kernel, tq=tq, tk=tk),
        grid_spec=grid_spec,
        out_shape=jax.ShapeDtypeStruct((bsz * t, MLA_HEADS * KV_LORA), BF16),
        compiler_params=_cparams("parallel", "arbitrary"),
        name="attn_prompt",
    )(qi, kj, qcat, kcat)


def _attn_sample_kernel(pt_ref, q_ref, kn_ref, lat_hbm, kr_hbm, o_ref, lat_buf, kr_buf, sems, m_scr, l_scr, acc_scr,
                        *, layer, n_pages, t):
    b = pl.program_id(0)
    nb = pl.num_programs(0)
    slot = b % 2
    rows = MLA_HEADS * t

    def page_copies(bb, sl, i):
        pg = pt_ref[bb, i]
        return (pltpu.make_async_copy(lat_hbm.at[layer, pg], lat_buf.at[sl, i * PAGE_SIZE:(i + 1) * PAGE_SIZE, :],
                                      sems.at[0, sl]),
                pltpu.make_async_copy(kr_hbm.at[layer, pg], kr_buf.at[sl, :, i * PAGE_SIZE:(i + 1) * PAGE_SIZE],
                                      sems.at[1, sl]))

    def start_fetch(bb, sl):
        for i in range(n_pages):
            for cp in page_copies(bb, sl, i):
                cp.start()

    @pl.when(b == 0)
    def _():
        start_fetch(0, 0)

    @pl.when(b + 1 < nb)
    def _():
        start_fetch(b + 1, 1 - slot)

    m_scr[...] = jnp.full((rows, LANES), -jnp.inf, F32)
    l_scr[...] = jnp.zeros((rows, LANES), F32)
    acc_scr[...] = jnp.zeros((rows, KV_LORA), F32)
    qs = jnp.concatenate([q_ref[:, h * KCAT:(h + 1) * KCAT] for h in range(MLA_HEADS)], axis=0).astype(BF16)

    for i in range(n_pages):
        for cp in page_copies(b, slot, i):
            cp.wait()

    klat = lat_buf[slot].astype(BF16)
    krope_t = kr_buf[slot].astype(BF16)
    s = _dot_nt(qs[:, :KV_LORA], klat) + _dot(qs[:, KV_LORA:KV_LORA + QK_ROPE], krope_t)
    _softmax_step(s, klat, m_scr, l_scr, acc_scr)

    kn = jnp.concatenate([kn_ref[...], jnp.zeros((LANES - t, KCAT), F32)], axis=0).astype(BF16)
    sn = _dot_nt(qs, kn)
    qpos = lax.broadcasted_iota(I32, (rows, LANES), 0) & (t - 1)
    kpos = lax.broadcasted_iota(I32, (rows, LANES), 1)
    _softmax_step(jnp.where(kpos <= qpos, sn, -jnp.inf), kn[:, :KV_LORA], m_scr, l_scr, acc_scr)
    inv = _lane_tile(1.0 / l_scr[...], KV_LORA // LANES)
    for h in range(MLA_HEADS):
        o_ref[:, h * KV_LORA:(h + 1) * KV_LORA] = acc_scr[h * t:(h + 1) * t, :] * inv[h * t:(h + 1) * t, :]


def _attn_sample(qcat, kcat, cache_lat, cache_kr, page_table, layer, bsz, t):
    n_pages = page_table.shape[1]
    past = n_pages * PAGE_SIZE
    rows = MLA_HEADS * t
    grid_spec = pltpu.PrefetchScalarGridSpec(
        num_scalar_prefetch=1,
        grid=(bsz,),
        in_specs=[pl.BlockSpec((None, t, MLA_HEADS * KCAT), lambda b, pt: (b, 0, 0)),
                  pl.BlockSpec((None, t, KCAT), lambda b, pt: (b, 0, 0)),
                  pl.BlockSpec(memory_space=pl.ANY), pl.BlockSpec(memory_space=pl.ANY)],
        out_specs=pl.BlockSpec((None, t, MLA_HEADS * KV_LORA), lambda b, pt: (b, 0, 0)),
        scratch_shapes=[pltpu.VMEM((2, past, KV_LORA), F32), pltpu.VMEM((2, QK_ROPE, past), F32),
                        pltpu.SemaphoreType.DMA((2, 2)),
                        pltpu.VMEM((rows, LANES), F32), pltpu.VMEM((rows, LANES), F32),
                        pltpu.VMEM((rows, KV_LORA), F32)],
    )
    out = pl.pallas_call(
        functools.partial(_attn_sample_kernel, layer=layer, n_pages=n_pages, t=t),
        grid_spec=grid_spec,
        out_shape=jax.ShapeDtypeStruct((bsz, t, MLA_HEADS * KV_LORA), F32),
        compiler_params=_cparams("arbitrary"),
        name="attn_sample",
    )(page_table, qcat.reshape(bsz, t, -1), kcat.reshape(bsz, t, -1), cache_lat, jnp.swapaxes(cache_kr, 2, 3))
    return out.reshape(bsz * t, -1)


def _mla_out_kernel(o_ref, wuv_ref, wo_ref, xr_ref, g_ref, b_ref, out_ref):
    o = o_ref[...].astype(BF16)
    parts = [_dot(o[:, h * KV_LORA:(h + 1) * KV_LORA], wuv_ref[h]).astype(BF16) for h in range(MLA_HEADS)]
    h = _dot(jnp.concatenate(parts, axis=1), wo_ref[...])
    out_ref[...] = _layer_norm(DEEPNORM_ALPHA * xr_ref[...] + h, g_ref[...], b_ref[...])


def _mla_out(o, wuv, wo, xres, g, b, tm):
    m = o.shape[0]
    full = lambda a: pl.BlockSpec(a.shape, lambda i: (0,) * a.ndim)
    row = lambda n: pl.BlockSpec((tm, n), lambda i: (i, 0))
    return pl.pallas_call(
        _mla_out_kernel,
        grid=(m // tm,),
        in_specs=[row(MLA_HEADS * KV_LORA), full(wuv), full(wo), row(D_MODEL), full(g), full(b)],
        out_specs=row(D_MODEL),
        out_shape=jax.ShapeDtypeStruct((m, D_MODEL), F32),
        compiler_params=_cparams("parallel"),
        name="mla_out",
    )(o, wuv, wo, xres, g, b)


def _route_kernel(x_ref, rh_ref, rl_ref, bias_ref, pa_ref, pb_ref, wa_ref, wb_ref, ng_ref, *, subs):
    tm = subs * MOE_SUB
    xh, xm = _split2(x_ref[...])
    rh = rh_ref[...]
    logits = _dot_nt(rh, xh) + _dot_nt(rh, xm) + _dot_nt(rl_ref[...], xh)
    aff = jax.nn.sigmoid(logits)
    sel = aff + bias_ref[...]
    sub = lax.broadcasted_iota(I32, (EXPERTS_PER_GROUP, tm), 0)
    for g in range(MOE_GROUPS):
        sl = slice(g * EXPERTS_PER_GROUP, (g + 1) * EXPERTS_PER_GROUP)
        sg, ag = sel[sl, :], aff[sl, :]
        m1 = jnp.max(sg, axis=0, keepdims=True)
        i1 = jnp.min(jnp.where(sg == m1, sub, EXPERTS_PER_GROUP), axis=0, keepdims=True)
        sg2 = jnp.where(sub == i1, -jnp.inf, sg)
        m2 = jnp.max(sg2, axis=0, keepdims=True)
        i2 = jnp.min(jnp.where(sg2 == m2, sub, EXPERTS_PER_GROUP), axis=0, keepdims=True)
        a1 = jnp.sum(jnp.where(sub == i1, ag, 0.0), axis=0, keepdims=True)
        a2 = jnp.sum(jnp.where(sub == i2, ag, 0.0), axis=0, keepdims=True)
        score = m1 + m2
        if g == 0:
            best, ea, eb, va, vb = score, i1, i2, a1, a2
        else:
            better = score > best
            best = jnp.where(better, score, best)
            ea = jnp.where(better, i1 + g * EXPERTS_PER_GROUP, ea)
            eb = jnp.where(better, i2 + g * EXPERTS_PER_GROUP, eb)
            va = jnp.where(better, a1, va)
            vb = jnp.where(better, a2, vb)
    den = va + vb
    wa_ref[...] = va / den
    wb_ref[...] = vb / den

    eid = lax.broadcasted_iota(I32, (N_EXPERTS, tm), 0)
    oh_a = jnp.where(eid == ea, 1.0, 0.0)
    oh_b = jnp.where(eid == eb, 1.0, 0.0)
    before = lax.broadcasted_iota(I32, (MOE_SUB, MOE_SUB), 0) < lax.broadcasted_iota(I32, (MOE_SUB, MOE_SUB), 1)
    upper = jnp.where(before, 1.0, 0.0).astype(BF16)
    lower = jnp.where(lax.broadcasted_iota(I32, (N_EXPERTS, N_EXPERTS), 1)
                      < lax.broadcasted_iota(I32, (N_EXPERTS, N_EXPERTS), 0), 1.0, 0.0).astype(BF16)
    lane = lax.broadcasted_iota(I32, (N_EXPERTS, LANES), 1)
    ng_all = jnp.zeros((N_EXPERTS, LANES), F32)
    pos_a, pos_b = [], []
    for s in range(subs):
        a = oh_a[:, s * MOE_SUB:(s + 1) * MOE_SUB]
        b = oh_b[:, s * MOE_SUB:(s + 1) * MOE_SUB]
        cnt_a = jnp.sum(a, axis=1, keepdims=True)
        cnt = cnt_a + jnp.sum(b, axis=1, keepdims=True)
        ng = jnp.floor((cnt + (MOE_GRAN - 1)) * (1.0 / MOE_GRAN))
        rows = jnp.broadcast_to(ng * MOE_GRAN, (N_EXPERTS, LANES)).astype(BF16)
        start = _dot(lower, rows)[:, :1]
        pos_a.append(jnp.sum(a * (start + _dot(a.astype(BF16), upper)), axis=0, keepdims=True))
        pos_b.append(jnp.sum(b * (start + cnt_a + _dot(b.astype(BF16), upper)), axis=0, keepdims=True))
        ng_all = jnp.where(lane == s, ng, ng_all)
    pa_ref[...] = jnp.concatenate(pos_a, axis=1).astype(I32)
    pb_ref[...] = jnp.concatenate(pos_b, axis=1).astype(I32)
    ng_ref[...] = ng_all


def _route(x, rh, rl, bias, subs):
    m = x.shape[0]
    tm = subs * MOE_SUB
    full = lambda a: pl.BlockSpec(a.shape, lambda i: (0, 0))
    lane_row = pl.BlockSpec((1, tm), lambda i: (0, i))
    row_i = jax.ShapeDtypeStruct((1, m), I32)
    row_f = jax.ShapeDtypeStruct((1, m), F32)
    return pl.pallas_call(
        functools.partial(_route_kernel, subs=subs),
        grid=(m // tm,),
        in_specs=[pl.BlockSpec((tm, D_MODEL), lambda i: (i, 0)), full(rh), full(rl), full(bias)],
        out_specs=[lane_row, lane_row, lane_row, lane_row,
                   pl.BlockSpec((None, N_EXPERTS, LANES), lambda i: (i, 0, 0))],
        out_shape=[row_i, row_i, row_f, row_f, jax.ShapeDtypeStruct((m // tm, N_EXPERTS, LANES), F32)],
        compiler_params=_cparams("parallel"),
        name="moe_route",
    )(x, rh, rl, bias)


def _moe_kernel(ng_ref, gs_ref, nit_ref, x_ref, pa_ref, pb_ref, wa_ref, wb_ref, wg_ref, wu_ref, wd_ref,
                g_ref, b_ref, o_ref, buf, *, ns, subs, eps):
    c = pl.program_id(0)
    st = pl.program_id(1)
    n_sort = ns // subs
    n_exp = N_EXPERTS // eps
    row_id = lax.broadcasted_iota(I32, (MOE_ROWS, MOE_SUB), 0)

    @pl.when(st < n_sort)
    def _():
        for k in range(subs):
            sl = slice(k * MOE_SUB, (k + 1) * MOE_SUB)
            hit = jnp.logical_or(row_id == pa_ref[:, sl], row_id == pb_ref[:, sl])
            perm = jnp.where(hit, 1.0, 0.0).astype(BF16)
            s = st * subs + k
            buf[s, 0:MOE_ROWS, :] = _dot(perm, x_ref[sl, :].astype(BF16)).astype(BF16)
            buf[s, MOE_ROWS:MOE_ROWS + MOE_GRAN, :] = jnp.zeros((MOE_GRAN, D_MODEL), BF16)

    @pl.when(jnp.logical_and(st >= n_sort, st < n_sort + n_exp))
    def _():
        e0 = (st - n_sort) * eps

        def body(w, carry):
            starts, xws = [], []
            for j in range(eps):
                row0s = []
                for s in range(ns):
                    idx = (c * ns + s) * N_EXPERTS + e0 + j
                    row0 = jnp.where(w < ng_ref[idx], gs_ref[idx] + w * MOE_GRAN, MOE_ROWS)
                    row0s.append(pl.multiple_of(row0, MOE_GRAN))
                starts.append(row0s)
                xws.append(jnp.concatenate([buf[s, pl.ds(row0s[s], MOE_GRAN), :] for s in range(ns)], axis=0))
            ys = []
            for j in range(eps):
                h = _silu(_dot(xws[j], wg_ref[j])) * _dot(xws[j], wu_ref[j])
                ys.append(_dot(h.astype(BF16), wd_ref[j]).astype(BF16))
            for j in range(eps):
                for s in range(ns):
                    buf[s, pl.ds(starts[j][s], MOE_GRAN), :] = ys[j][s * MOE_GRAN:(s + 1) * MOE_GRAN, :]
            return carry

        trips = nit_ref[c * N_EXPERTS + e0]
        for j in range(1, eps):
            trips = jnp.maximum(trips, nit_ref[c * N_EXPERTS + e0 + j])
        lax.fori_loop(0, trips, body, 0)

    @pl.when(st >= n_sort + n_exp)
    def _():
        for k in range(subs):
            sl = slice(k * MOE_SUB, (k + 1) * MOE_SUB)
            gate = (jnp.where(row_id == pa_ref[:, sl], wa_ref[:, sl], 0.0)
                    + jnp.where(row_id == pb_ref[:, sl], wb_ref[:, sl], 0.0)).astype(BF16)
            s = (st - n_sort - n_exp) * subs + k
            y = _dot_tn(gate, buf[s, 0:MOE_ROWS, :])
            o_ref[sl, :] = _layer_norm(DEEPNORM_ALPHA * x_ref[sl, :] + y, g_ref[...], b_ref[...])


def _hier_moe_ln(x, prm, g, b, ns):
    m = x.shape[0]
    subs, eps = MOE_SUBS_PER_STEP, MOE_EXPERTS_PER_STEP
    tm = subs * MOE_SUB
    chunk = ns * MOE_SUB
    n_chunks = m // chunk
    n_sort = ns // subs
    n_exp = N_EXPERTS // eps
    pa, pb, wa, wb, ng = _route(x, prm["router_hi"], prm["router_lo"], prm["router_bias"], subs)
    ng = jnp.transpose(ng[:, :, :subs].astype(I32), (0, 2, 1)).reshape(n_chunks, ns, N_EXPERTS)
    gstart = (jnp.cumsum(ng, axis=-1) - ng) * MOE_GRAN
    nit = jnp.max(ng, axis=1)

    def tok_idx(c, st, *_):
        phase = jnp.where(st < n_sort, st, jnp.where(st >= n_sort + n_exp, st - n_sort - n_exp, n_sort - 1))
        return c * n_sort + phase

    def exp_idx(c, st, *_):
        return (prm["layer"], jnp.clip(st - n_sort, 0, n_exp - 1), 0, 0)

    lane_row = pl.BlockSpec((1, tm), lambda c, st, *_: (0, tok_idx(c, st)))
    full = lambda a: pl.BlockSpec(a.shape, lambda c, st, *_: (0, 0))
    grid_spec = pltpu.PrefetchScalarGridSpec(
        num_scalar_prefetch=3,
        grid=(n_chunks, 2 * n_sort + n_exp),
        in_specs=[pl.BlockSpec((tm, D_MODEL), lambda c, st, *_: (tok_idx(c, st), 0)),
                  lane_row, lane_row, lane_row, lane_row,
                  pl.BlockSpec((None, eps, D_MODEL, EXPERT_FF), exp_idx),
                  pl.BlockSpec((None, eps, D_MODEL, EXPERT_FF), exp_idx),
                  pl.BlockSpec((None, eps, EXPERT_FF, D_MODEL), exp_idx),
                  full(g), full(b)],
        out_specs=pl.BlockSpec((tm, D_MODEL),
                               lambda c, st, *_: (c * n_sort + jnp.clip(st - n_sort - n_exp, 0, n_sort - 1), 0)),
        scratch_shapes=[pltpu.VMEM((ns, MOE_ROWS + MOE_GRAN, D_MODEL), BF16)],
    )
    return pl.pallas_call(
        functools.partial(_moe_kernel, ns=ns, subs=subs, eps=eps),
        grid_spec=grid_spec,
        out_shape=jax.ShapeDtypeStruct((m, D_MODEL), F32),
        compiler_params=_cparams("arbitrary", "arbitrary"),
        name="moe_experts",
    )(ng.reshape(-1), gstart.reshape(-1), nit.reshape(-1), x, pa, pb, wa, wb,
      prm["w_gate"], prm["w_up"], prm["w_down"], g, b)


def _pad_cols(a, n):
    return jnp.pad(a, ((0, 0), (0, n - a.shape[1])))


def _ssd_params(w_in, conv_w, conv_b, dt_bias, a_log, d_skip, norm_w, w_out):
    wdt = _pad_cols(w_in[:, D_INNER + CONV_DIM:], LANES)
    wdh = wdt.astype(BF16)
    head_of_col = jnp.arange(D_INNER, dtype=I32) // SSD_HEAD_DIM
    expand = (jnp.arange(LANES, dtype=I32)[:, None] == head_of_col[None, :]).astype(BF16)
    return {
        "wz": w_in[:, :D_INNER].astype(BF16),
        "wx": w_in[:, D_INNER:D_INNER + CONV_DIM].astype(BF16),
        "wdh": wdh,
        "wdl": (wdt - wdh.astype(F32)).astype(BF16),
        "conv_w": conv_w,
        "conv_b": conv_b[None, :],
        "dt_bias": _pad_cols(dt_bias[None, :], LANES),
        "a_log": _pad_cols(a_log[None, :], LANES),
        "d_skip": jnp.repeat(d_skip, SSD_HEAD_DIM)[None, :],
        "norm_w": norm_w[None, :],
        "expand": expand,
        "w_out": w_out.astype(BF16),
    }


def _mla_params(w_dqkv, q_norm, kv_norm, w_uq, w_uk, w_uv, w_o):
    wr = jnp.pad(w_uq[:, :, QK_NOPE:], ((0, 0), (0, 0), (0, LANES - QK_ROPE)))
    return {
        "w_down": _pad_cols(w_dqkv, DCOLS).astype(BF16),
        "q_norm": q_norm[None, :],
        "kv_norm": kv_norm[None, :],
        "w_qn": w_uq[:, :, :QK_NOPE].reshape(Q_LORA, MLA_HEADS * QK_NOPE).astype(BF16),
        "w_qr": wr.reshape(Q_LORA, MLA_HEADS * LANES).astype(BF16),
        "w_uk_t": jnp.transpose(w_uk, (1, 2, 0)).astype(BF16),
        "w_uv": jnp.transpose(w_uv, (1, 0, 2)).astype(BF16),
        "w_o": w_o.astype(BF16),
    }


def _moe_params(layer, router, router_bias, w_gate, w_up, w_down):
    rt = router.T
    rh = rt.astype(BF16)
    return {
        "layer": layer,
        "router_hi": rh,
        "router_lo": (rt - rh.astype(F32)).astype(BF16),
        "router_bias": router_bias[:, None],
        "w_gate": w_gate,
        "w_up": w_up,
        "w_down": w_down,
    }


def _rope_tables(pos):
    half = QK_ROPE // 2
    inv_freq = ROPE_THETA ** (-jnp.arange(half, dtype=F32) / half)
    ang = pos.astype(F32)[:, None] * inv_freq[None, :]
    cos, sin = jnp.cos(ang), jnp.sin(ang)
    zeros = jnp.zeros((pos.shape[0], LANES - QK_ROPE), F32)
    return jnp.concatenate([cos, cos, zeros], axis=1), jnp.concatenate([-sin, sin, zeros], axis=1)


def _tile(m, want):
    return min(m, want)


def kernel(x_prompt, x_sample, cache_mla_latent, cache_mla_krope, state_ssd, state_conv, page_table,
           ln_mix_g, ln_mix_b, ln_ffn_g, ln_ffn_b,
           ssd_w_in, ssd_conv_w, ssd_conv_b, ssd_dt_bias, ssd_a_log, ssd_d_skip, ssd_norm_w, ssd_w_out,
           mla_w_dqkv, mla_q_norm, mla_kv_norm, mla_w_uq, mla_w_uk, mla_w_uv, mla_w_o,
           moe_router, moe_router_bias, moe_w_gate, moe_w_up, moe_w_down):
    bp, tp, _ = x_prompt.shape
    bs, ts, _ = x_sample.shape
    past_len = page_table.shape[1] * PAGE_SIZE
    xp = x_prompt.reshape(bp * tp, D_MODEL)
    xs = x_sample.reshape(bs * ts, D_MODEL)
    lp = min(SSD_CHUNK, tp)

    cos_p, sin_p = _rope_tables(jnp.arange(tp))
    cos_s, sin_s = _rope_tables(past_len + jnp.arange(ts))
    cos_s = jnp.tile(cos_s, (bs, 1))
    sin_s = jnp.tile(sin_s, (bs, 1))

    w_gate_b, w_up_b, w_down_b = moe_w_gate.astype(BF16), moe_w_up.astype(BF16), moe_w_down.astype(BF16)
    p_lat, p_kr, p_ssm, p_conv = [], [], [], []
    s_lat, s_kr, s_ssm, s_conv = [], [], [], []
    for i in range(DEPTH):
        j = i // 2
        g_mix, b_mix = ln_mix_g[i][None, :], ln_mix_b[i][None, :]
        if i % 2 == 0:
            prm = _ssd_params(ssd_w_in[j], ssd_conv_w[j], ssd_conv_b[j], ssd_dt_bias[j], ssd_a_log[j],
                              ssd_d_skip[j], ssd_norm_w[j], ssd_w_out[j])
            tm = _tile(tp, 256)
            zs, dt, xact, bc, tails = _ssd_inproj(xp, prm, BF16, tm, seq_len=tp)
            y, sp = _ssd_core([xact, bc], zs, dt, prm, bp, tp, lp, BF16)
            cp = tails.reshape(bp, tp // tm, 8, CONV_DIM)[:, -1, 8 - (SSD_CONV - 1):, :]
            xp = _proj_ln(y, prm["w_out"], xp, g_mix, b_mix, _tile(xp.shape[0], 512))
            zs, dt, xbc = _ssd_inproj(xs, prm, F32, _tile(xs.shape[0], 256))
            y, cs, ss = _ssd_core([xbc], zs, dt, prm, bs, ts, ts, F32, init=(state_conv, state_ssd, j))
            xs = _proj_ln(y, prm["w_out"], xs, g_mix, b_mix, _tile(xs.shape[0], 512))
            p_conv.append(cp)
            p_ssm.append(sp)
            s_conv.append(cs)
            s_ssm.append(ss)
        else:
            prm = _mla_params(mla_w_dqkv[j], mla_q_norm[j], mla_kv_norm[j], mla_w_uq[j], mla_w_uk[j],
                              mla_w_uv[j], mla_w_o[j])
            tm = _tile(tp, 512)
            cq, lat, kr, kcat = _mla_down(xp, prm["w_down"], prm["q_norm"], prm["kv_norm"], cos_p, sin_p, BF16, tm)
            qcat = _mla_q(cq, prm["w_qn"], prm["w_qr"], prm["w_uk_t"], cos_p, sin_p, BF16, tm)
            o = _attn_prompt(qcat, kcat, bp, tp, tm, tm)
            xp = _mla_out(o, prm["w_uv"], prm["w_o"], xp, g_mix, b_mix, tm)
            p_lat.append(lat.reshape(bp, tp, KV_LORA))
            p_kr.append(kr.reshape(bp, tp, QK_ROPE))
            tm = _tile(xs.shape[0], 512)
            cq, lat, kr, kcat = _mla_down(xs, prm["w_down"], prm["q_norm"], prm["kv_norm"], cos_s, sin_s, F32, tm)
            qcat = _mla_q(cq, prm["w_qn"], prm["w_qr"], prm["w_uk_t"], cos_s, sin_s, F32, tm)
            o = _attn_sample(qcat, kcat, cache_mla_latent, cache_mla_krope, page_table, j, bs, ts)
            xs = _mla_out(o, prm["w_uv"], prm["w_o"], xs, g_mix, b_mix, tm)
            s_lat.append(lat.reshape(bs, ts, KV_LORA))
            s_kr.append(kr.reshape(bs, ts, QK_ROPE))
        mprm = _moe_params(i, moe_router[i], moe_router_bias[i], w_gate_b, w_up_b, w_down_b)
        g_ffn, b_ffn = ln_ffn_g[i][None, :], ln_ffn_b[i][None, :]
        xp = _hier_moe_ln(xp, mprm, g_ffn, b_ffn, min(8, xp.shape[0] // MOE_SUB))
        xs = _hier_moe_ln(xs, mprm, g_ffn, b_ffn, min(8, xs.shape[0] // MOE_SUB))

    return (xp.reshape(bp, tp, D_MODEL), xs.reshape(bs, ts, D_MODEL),
            jnp.stack(p_lat), jnp.stack(p_kr), jnp.stack(p_ssm), jnp.stack(p_conv),
            jnp.stack(s_lat), jnp.stack(s_kr), jnp.stack(s_ssm), jnp.stack(s_conv))
```

```python
import functools

import jax
import jax.numpy as jnp
from jax import lax
from jax.experimental import pallas as pl
from jax.experimental.pallas import tpu as pltpu

F32 = jnp.float32
BF16 = jnp.bfloat16
I32 = jnp.int32

D_MODEL = 1024
DEPTH = 4
PAGE_SIZE = 128
D_INNER = 2048
SSD_HEAD_DIM = 64
SSD_HEADS = 32
SSD_GROUPS = 4
SSD_STATE = 128
SSD_CONV = 4
SSD_CHUNK = 128
CONV_DIM = D_INNER + 2 * SSD_GROUPS * SSD_STATE
GROUP_INNER = D_INNER // SSD_GROUPS
MLA_HEADS = 8
Q_LORA = 384
KV_LORA = 256
QK_NOPE = 128
QK_ROPE = 64
V_HEAD = 128
ROPE_THETA = 10000.0
MOE_GROUPS = 4
EXPERTS_PER_GROUP = 8
N_EXPERTS = 32
EXPERT_FF = 256
DEEPNORM_ALPHA = (2.0 * DEPTH) ** 0.25
LN_EPS = 1e-5
RMS_EPS = 1e-6
ATTN_SCALE = (QK_NOPE + QK_ROPE) ** -0.5

LANES = 128
KCAT = KV_LORA + LANES
DCOLS = Q_LORA + KV_LORA + LANES
MOE_SUB = 256
MOE_GRAN = 16
MOE_ROWS = 2 * MOE_SUB + N_EXPERTS * MOE_GRAN
MOE_SUBS_PER_STEP = 4
MOE_EXPERTS_PER_STEP = 4
VMEM_LIMIT = 56 * 1024 * 1024


def _cparams(*sem):
    return pltpu.CompilerParams(dimension_semantics=sem, vmem_limit_bytes=VMEM_LIMIT)


def _dot(a, b):
    return jnp.dot(a, b, preferred_element_type=F32)


def _dot_nt(a, b):
    return lax.dot_general(a, b, (((1,), (1,)), ((), ())), preferred_element_type=F32)


def _dot_tn(a, b):
    return lax.dot_general(a, b, (((0,), (0,)), ((), ())), preferred_element_type=F32)


def _split2(x):
    hi = x.astype(BF16)
    return hi, (x - hi.astype(F32)).astype(BF16)


def _split3(x):
    hi = x.astype(BF16)
    r = x - hi.astype(F32)
    mid = r.astype(BF16)
    return hi, mid, (r - mid.astype(F32)).astype(BF16)


def _silu(x):
    return x * jax.nn.sigmoid(x)


def _softplus(x):
    return jnp.maximum(x, 0.0) + jnp.log(1.0 + jnp.exp(-jnp.abs(x)))


def _layer_norm(v, g, b):
    mu = jnp.mean(v, axis=-1, keepdims=True)
    d = v - mu
    var = jnp.mean(d * d, axis=-1, keepdims=True)
    return d * lax.rsqrt(var + LN_EPS) * g + b


def _rms_norm(v, g):
    return v * lax.rsqrt(jnp.mean(v * v, axis=-1, keepdims=True) + RMS_EPS) * g


def _rope128(x, cos, sin):
    lane = lax.broadcasted_iota(I32, x.shape, 1)
    partner = jnp.where(lane < QK_ROPE // 2, pltpu.roll(x, LANES - QK_ROPE // 2, 1), pltpu.roll(x, QK_ROPE // 2, 1))
    return x * cos + partner * sin


def _causal_conv(xraw, xp_scr, cw_ref, cb_ref, n):
    xp_scr[8:8 + n, :] = xraw
    xp = xp_scr[...]
    acc = cb_ref[...]
    for k in range(SSD_CONV - 1):
        acc = acc + pltpu.roll(xp, SSD_CONV - 1 - k, 0)[8:8 + n, :] * cw_ref[k:k + 1, :]
    acc = acc + xraw * cw_ref[SSD_CONV - 1:SSD_CONV, :]
    return _silu(acc)


def _ssd_inproj_kernel(*refs, conv, tiles_per_seq):
    if conv:
        (x_ref, wz_ref, wx_ref, wdh_ref, wdl_ref, dtb_ref, cw_ref, cb_ref,
         zs_ref, dt_ref, xs_ref, bc_ref, tail_ref, xp_scr) = refs
    else:
        x_ref, wz_ref, wx_ref, wdh_ref, wdl_ref, dtb_ref, zs_ref, dt_ref, xbc_ref = refs
    if conv:
        @pl.when(pl.program_id(0) % tiles_per_seq == 0)
        def _():
            xp_scr[0:8, :] = jnp.zeros((8, CONV_DIM), F32)

    xh, xm = _split2(x_ref[...])
    xbc = _dot(xh, wx_ref[...])
    if conv:
        tm = xbc.shape[0]
        act = _causal_conv(xbc, xp_scr, cw_ref, cb_ref, tm)
        tail = xp_scr[tm:tm + 8, :]
        tail_ref[...] = tail
        xp_scr[0:8, :] = tail
        xs_ref[...] = act[:, :D_INNER]
        bc_ref[...] = act[:, D_INNER:].astype(BF16)
    else:
        xbc_ref[...] = xbc
    zs_ref[...] = _silu(_dot(xh, wz_ref[...])).astype(zs_ref.dtype)
    wdh = wdh_ref[...]
    dt_ref[...] = _softplus(_dot(xh, wdh) + _dot(xm, wdh) + _dot(xh, wdl_ref[...]) + dtb_ref[...])


def _ssd_inproj(x, prm, z_dtype, tm, seq_len=None):
    m = x.shape[0]
    conv = seq_len is not None
    full = lambda a: pl.BlockSpec(a.shape, lambda i: (0, 0))
    row = lambda n: pl.BlockSpec((tm, n), lambda i: (i, 0))
    args = [x, prm["wz"], prm["wx"], prm["wdh"], prm["wdl"], prm["dt_bias"]]
    out_specs = [row(D_INNER), row(LANES)]
    out_shape = [jax.ShapeDtypeStruct((m, D_INNER), z_dtype), jax.ShapeDtypeStruct((m, LANES), F32)]
    scratch = []
    if conv:
        args += [prm["conv_w"], prm["conv_b"]]
        out_specs += [row(D_INNER), row(CONV_DIM - D_INNER), pl.BlockSpec((None, 8, CONV_DIM), lambda i: (i, 0, 0))]
        out_shape += [jax.ShapeDtypeStruct((m, D_INNER), F32), jax.ShapeDtypeStruct((m, CONV_DIM - D_INNER), BF16),
                      jax.ShapeDtypeStruct((m // tm, 8, CONV_DIM), F32)]
        scratch = [pltpu.VMEM((tm + 8, CONV_DIM), F32)]
    else:
        out_specs += [row(CONV_DIM)]
        out_shape += [jax.ShapeDtypeStruct((m, CONV_DIM), F32)]
    return pl.pallas_call(
        functools.partial(_ssd_inproj_kernel, conv=conv, tiles_per_seq=(seq_len // tm if conv else 1)),
        grid=(m // tm,),
        in_specs=[row(D_MODEL)] + [full(a) for a in args[1:]],
        out_specs=out_specs,
        out_shape=out_shape,
        scratch_shapes=scratch,
        compiler_params=_cparams("arbitrary"),
        name="ssd_inproj_conv" if conv else "ssd_inproj",
    )(*args)


def _ssd_kernel(*refs, L, has_init, has_stacked):
    if has_init:
        if has_stacked:
            refs = refs[:11] + refs[12:]
        (xbc_ref, zs_ref, dt_ref, cw_ref, cb_ref, alog_ref, dskip_ref, nw_ref, e_ref,
         conv0_ref, ssm0_ref, y_ref, convo_ref, sso_ref, xp_scr, h_scr) = refs
    else:
        (xs_ref, bc_ref, zs_ref, dt_ref, alog_ref, dskip_ref, nw_ref, e_ref, y_ref, sso_ref, h_scr) = refs
    c = pl.program_id(1)
    nc = pl.num_programs(1)
    LK = max(L, LANES)
    K1 = SSD_CONV - 1
    HPG = SSD_HEADS // SSD_GROUPS
    NBC = SSD_GROUPS * SSD_STATE

    @pl.when(c == 0)
    def _():
        if has_init:
            xp_scr[8 - K1:8, :] = conv0_ref[...]
            h_scr[...] = ssm0_ref[...].reshape(D_INNER, SSD_STATE)
        else:
            h_scr[...] = jnp.zeros((D_INNER, SSD_STATE), F32)

    def pad_rows(a):
        if L == LK:
            return a
        return jnp.concatenate([a, jnp.zeros((LK - L, a.shape[1]), a.dtype)], axis=0)

    if has_init:
        act = _causal_conv(xbc_ref[...], xp_scr, cw_ref, cb_ref, L)
        tail = xp_scr[L + 8 - K1:L + 8, :]

        @pl.when(c == nc - 1)
        def _():
            convo_ref[...] = tail

        xp_scr[8 - K1:8, :] = tail
        xs = act[:, :D_INNER]
        bm = act[:, D_INNER:D_INNER + NBC].astype(BF16)
        cm = act[:, D_INNER + NBC:].astype(BF16)
    else:
        xs = xs_ref[...]
        bm = bc_ref[:, :NBC]
        cm = bc_ref[:, NBC:]

    dtv = dt_ref[...]
    d_a = dtv * (-jnp.exp(alog_ref[...]))

    row = lax.broadcasted_iota(I32, (L, LK), 0)
    col = lax.broadcasted_iota(I32, (L, LK), 1)
    causal = col <= row
    tril = jnp.where(causal, 1.0, 0.0).astype(BF16)
    cumc = sum(_dot(tril, p) for p in _split3(pad_rows(d_a)))
    ir = lax.broadcasted_iota(I32, (LANES, LANES), 0)
    ic = lax.broadcasted_iota(I32, (LANES, LANES), 1)
    ident = jnp.where(ir == ic, 1.0, 0.0).astype(BF16)
    cumr = sum(_dot_nt(ident, p) for p in _split3(pad_rows(cumc)))

    ecum = jnp.exp(cumc)
    last = cumc[L - 1:L, :]
    ws = jnp.exp(last - cumc)
    elc = jnp.exp(cumr[:, L - 1:L])
    e_mat = e_ref[...]
    dt_x = _dot(dtv.astype(BF16), e_mat)
    ecum_x = _dot(ecum.astype(BF16), e_mat)
    ws_x = _dot(ws.astype(BF16), e_mat)

    xdt = xs * dt_x
    xdt_b = xdt.astype(BF16)
    xw_p = pad_rows((xdt * ws_x).astype(BF16))
    lo_half = lax.broadcasted_iota(I32, (LK, LANES), 1) < SSD_HEAD_DIM
    zero_b = jnp.zeros((LK, LANES), BF16)
    dskip = dskip_ref[...]
    neg_inf = jnp.float32(-jnp.inf)

    ys = []
    for g in range(SSD_GROUPS):
        bc_g = bm[:, g * SSD_STATE:(g + 1) * SSD_STATE]
        cc_g = cm[:, g * SSD_STATE:(g + 1) * SSD_STATE]
        bc_p = pad_rows(bc_g)
        cb = _dot_nt(cc_g, bc_p)
        h_g = h_scr[g * GROUP_INNER:(g + 1) * GROUP_INNER, :]
        ys_g = _dot_nt(cc_g, h_g.astype(BF16))
        for q in range(HPG // 2):
            ha = g * HPG + 2 * q
            c0 = (g * (HPG // 2) + q) * LANES
            dec = []
            for hh in (ha, ha + 1):
                seg = cumc[:, hh:hh + 1] - cumr[hh:hh + 1, :]
                dec.append(cb * jnp.exp(jnp.where(causal, seg, neg_inf)))
            m_cat = jnp.concatenate(dec, axis=1).astype(BF16)
            x_p = pad_rows(xdt_b[:, c0:c0 + LANES])
            xbd = jnp.concatenate([jnp.where(lo_half, x_p, zero_b), jnp.where(lo_half, zero_b, x_p)], axis=0)
            y_p = _dot(m_cat, xbd)
            y_p = y_p + ys_g[:, q * LANES:(q + 1) * LANES] * ecum_x[:, c0:c0 + LANES]
            y_p = y_p + dskip[:, c0:c0 + LANES] * xs[:, c0:c0 + LANES]
            ys.append(y_p)
        upd = _dot_tn(xw_p[:, g * GROUP_INNER:(g + 1) * GROUP_INNER], bc_p)
        scaled = [h_g[j * SSD_HEAD_DIM:(j + 1) * SSD_HEAD_DIM, :] * elc[g * HPG + j:g * HPG + j + 1, :]
                  for j in range(HPG)]
        h_scr[g * GROUP_INNER:(g + 1) * GROUP_INNER, :] = jnp.concatenate(scaled, axis=0) + upd

    y = jnp.concatenate(ys, axis=1)
    gated = y * zs_ref[...].astype(F32)
    outs = []
    for g in range(SSD_GROUPS):
        gg = gated[:, g * GROUP_INNER:(g + 1) * GROUP_INNER]
        outs.append(gg * lax.rsqrt(jnp.mean(gg * gg, axis=-1, keepdims=True) + RMS_EPS))
    y_ref[...] = (jnp.concatenate(outs, axis=1) * nw_ref[...]).astype(y_ref.dtype)

    @pl.when(c == nc - 1)
    def _():
        sso_ref[...] = h_scr[...].reshape(SSD_HEADS, SSD_HEAD_DIM, SSD_STATE)


def _ssd_core(seq_in, zs, dt, prm, bsz, t, L, y_dtype, init=None, stacked=None):
    nc = t // L
    m = bsz * t
    has_init = init is not None
    tok = lambda n: pl.BlockSpec((L, n), lambda b, c: (b * nc + c, 0))
    full = lambda a: pl.BlockSpec(a.shape, lambda b, c: (0, 0))
    params = [prm["a_log"], prm["d_skip"], prm["norm_w"], prm["expand"]]
    if has_init:
        params = [prm["conv_w"], prm["conv_b"]] + params
    in_specs = [tok(a.shape[1]) for a in seq_in] + [tok(D_INNER), tok(LANES)] + [full(a) for a in params]
    args = list(seq_in) + [zs, dt] + params
    out_specs = [tok(D_INNER)]
    out_shape = [jax.ShapeDtypeStruct((m, D_INNER), y_dtype)]
    scratch = []
    if has_init:
        conv0, ssm0, layer = init
        in_specs += [pl.BlockSpec((None, None, SSD_CONV - 1, CONV_DIM), lambda b, c: (layer, b, 0, 0)),
                     pl.BlockSpec((None, None, SSD_HEADS, SSD_HEAD_DIM, SSD_STATE), lambda b, c: (layer, b, 0, 0, 0))]
        args += [conv0, ssm0]
        out_specs += [pl.BlockSpec((None, SSD_CONV - 1, CONV_DIM), lambda b, c: (b, 0, 0))]
        out_shape += [jax.ShapeDtypeStruct((bsz, SSD_CONV - 1, CONV_DIM), F32)]
        scratch = [pltpu.VMEM((L + 8, CONV_DIM), F32)]
    aliases = {}
    if has_init:
        n_layers = ssm0.shape[0]
        out_specs += [pl.BlockSpec((None, None, SSD_HEADS, SSD_HEAD_DIM, SSD_STATE), lambda b, c: (layer, b, 0, 0, 0))]
        out_shape += [jax.ShapeDtypeStruct((n_layers, bsz, SSD_HEADS, SSD_HEAD_DIM, SSD_STATE), F32)]
        if stacked is not None:
            in_specs += [pl.BlockSpec(memory_space=pl.ANY)]
            args += [stacked]
            aliases = {len(args) - 1: len(out_shape) - 1}
    else:
        out_specs += [pl.BlockSpec((None, SSD_HEADS, SSD_HEAD_DIM, SSD_STATE), lambda b, c: (b, 0, 0, 0))]
        out_shape += [jax.ShapeDtypeStruct((bsz, SSD_HEADS, SSD_HEAD_DIM, SSD_STATE), F32)]
    return pl.pallas_call(
        functools.partial(_ssd_kernel, L=L, has_init=has_init, has_stacked=stacked is not None),
        grid=(bsz, nc),
        in_specs=in_specs,
        out_specs=out_specs,
        out_shape=out_shape,
        scratch_shapes=scratch + [pltpu.VMEM((D_INNER, SSD_STATE), F32)],
        input_output_aliases=aliases,
        compiler_params=_cparams("parallel", "arbitrary"),
        name="ssd_core_init" if has_init else "ssd_core",
    )(*args)


def _proj_ln_kernel(y_ref, w_ref, xr_ref, g_ref, b_ref, o_ref):
    h = _dot(y_ref[...].astype(BF16), w_ref[...])
    o_ref[...] = _layer_norm(DEEPNORM_ALPHA * xr_ref[...] + h, g_ref[...], b_ref[...])


def _proj_ln(y, w, xres, g, b, tm):
    m, k = y.shape
    full = lambda a: pl.BlockSpec(a.shape, lambda i: (0, 0))
    row = lambda n: pl.BlockSpec((tm, n), lambda i: (i, 0))
    return pl.pallas_call(
        _proj_ln_kernel,
        grid=(m // tm,),
        in_specs=[row(k), full(w), row(D_MODEL), full(g), full(b)],
        out_specs=row(D_MODEL),
        out_shape=jax.ShapeDtypeStruct((m, D_MODEL), F32),
        compiler_params=_cparams("parallel"),
        name="proj_ln",
    )(y, w, xres, g, b)


def _mla_down_kernel(x_ref, w_ref, qn_ref, kvn_ref, cos_ref, sin_ref, cq_ref, ckv_ref, kr_ref, kcat_ref):
    c = _dot(x_ref[...].astype(BF16), w_ref[...])
    cq = _rms_norm(c[:, :Q_LORA], qn_ref[...])
    ckv = _rms_norm(c[:, Q_LORA:Q_LORA + KV_LORA], kvn_ref[...])
    kr = _rope128(c[:, Q_LORA + KV_LORA:], cos_ref[...], sin_ref[...])
    cq_ref[...] = cq.astype(BF16)
    ckv_ref[...] = ckv
    kr_ref[...] = kr[:, :QK_ROPE]
    kcat_ref[...] = jnp.concatenate([ckv, kr], axis=1).astype(kcat_ref.dtype)


def _mla_down(x, w, qn, kvn, cos, sin, kcat_dtype, tm):
    m = x.shape[0]
    nper = cos.shape[0] // tm
    full = lambda a: pl.BlockSpec(a.shape, lambda i: (0, 0))
    row = lambda n: pl.BlockSpec((tm, n), lambda i: (i, 0))
    tab = pl.BlockSpec((tm, LANES), lambda i: (i % nper, 0))
    return pl.pallas_call(
        _mla_down_kernel,
        grid=(m // tm,),
        in_specs=[row(D_MODEL), full(w), full(qn), full(kvn), tab, tab],
        out_specs=[row(Q_LORA), row(KV_LORA), row(QK_ROPE), row(KCAT)],
        out_shape=[jax.ShapeDtypeStruct((m, Q_LORA), BF16),
                   jax.ShapeDtypeStruct((m, KV_LORA), F32),
                   jax.ShapeDtypeStruct((m, QK_ROPE), F32),
                   jax.ShapeDtypeStruct((m, KCAT), kcat_dtype)],
        compiler_params=_cparams("parallel"),
        name="mla_down",
    )(x, w, qn, kvn, cos, sin)


def _mla_q_kernel(cq_ref, wn_ref, wr_ref, wuk_ref, cos_ref, sin_ref, q_ref):
    cq = cq_ref[...]
    qn = _dot(cq, wn_ref[...]).astype(BF16)
    qr = _dot(cq, wr_ref[...])
    cos = cos_ref[...]
    sin = sin_ref[...]
    for h in range(MLA_HEADS):
        ql = _dot(qn[:, h * QK_NOPE:(h + 1) * QK_NOPE], wuk_ref[h]) * ATTN_SCALE
        rp = _rope128(qr[:, h * LANES:(h + 1) * LANES], cos, sin) * ATTN_SCALE
        q_ref[:, h * KCAT:h * KCAT + KV_LORA] = ql.astype(q_ref.dtype)
        q_ref[:, h * KCAT + KV_LORA:(h + 1) * KCAT] = rp.astype(q_ref.dtype)


def _mla_q(cq, wn, wr, wuk, cos, sin, q_dtype, tm):
    m = cq.shape[0]
    nper = cos.shape[0] // tm
    full2 = lambda a: pl.BlockSpec(a.shape, lambda i: (0, 0))
    tab = pl.BlockSpec((tm, LANES), lambda i: (i % nper, 0))
    return pl.pallas_call(
        _mla_q_kernel,
        grid=(m // tm,),
        in_specs=[pl.BlockSpec((tm, Q_LORA), lambda i: (i, 0)), full2(wn), full2(wr),
                  pl.BlockSpec(wuk.shape, lambda i: (0, 0, 0)), tab, tab],
        out_specs=pl.BlockSpec((tm, MLA_HEADS * KCAT), lambda i: (i, 0)),
        out_shape=jax.ShapeDtypeStruct((m, MLA_HEADS * KCAT), q_dtype),
        compiler_params=_cparams("parallel"),
        name="mla_q",
    )(cq, wn, wr, wuk, cos, sin)


def _lane_tile(a, n):
    return a if n == 1 else jnp.concatenate([a] * n, axis=1)


def _softmax_step(s, v, m_scr, l_scr, acc_scr):
    m_prev = m_scr[...]
    m_new = jnp.maximum(m_prev, jnp.max(s, axis=-1, keepdims=True))
    corr = jnp.exp(m_prev - m_new)
    p = jnp.exp(s - _lane_tile(m_new, s.shape[1] // LANES))
    l_scr[...] = corr * l_scr[...] + jnp.sum(p, axis=-1, keepdims=True)
    acc_scr[...] = _lane_tile(corr, KV_LORA // LANES) * acc_scr[...] + _dot(p.astype(BF16), v)
    m_scr[...] = m_new


def _attn_prompt_kernel(qi_ref, kj_ref, q_ref, k_ref, o_ref, qs_scr, m_scr, l_scr, acc_scr, *, tq, tk):
    p = pl.program_id(1)
    i = qi_ref[p]
    j = kj_ref[p]
    rows = MLA_HEADS * tq

    @pl.when(j == 0)
    def _():
        for h in range(MLA_HEADS):
            qs_scr[h * tq:(h + 1) * tq, :] = q_ref[:, h * KCAT:(h + 1) * KCAT]
        m_scr[...] = jnp.full((rows, LANES), -jnp.inf, F32)
        l_scr[...] = jnp.zeros((rows, LANES), F32)
        acc_scr[...] = jnp.zeros((rows, KV_LORA), F32)

    k = k_ref[...]
    s = _dot_nt(qs_scr[...], k)
    v = k[:, :KV_LORA]
    crosses = (j + 1) * tk - 1 > i * tq

    @pl.when(crosses)
    def _():
        qpos = (lax.broadcasted_iota(I32, (rows, tk), 0) & (tq - 1)) + i * tq
        kpos = lax.broadcasted_iota(I32, (rows, tk), 1) + j * tk
        _softmax_step(jnp.where(kpos <= qpos, s, -jnp.inf), v, m_scr, l_scr, acc_scr)

    @pl.when(jnp.logical_not(crosses))
    def _():
        _softmax_step(s, v, m_scr, l_scr, acc_scr)

    @pl.when(j == ((i + 1) * tq - 1) // tk)
    def _():
        inv = _lane_tile(1.0 / l_scr[...], KV_LORA // LANES)
        for h in range(MLA_HEADS):
            o_ref[:, h * KV_LORA:(h + 1) * KV_LORA] = (
                acc_scr[h * tq:(h + 1) * tq, :] * inv[h * tq:(h + 1) * tq, :]).astype(o_ref.dtype)


def _attn_prompt(qcat, kcat, bsz, t, tq, tk):
    nq, nk = t // tq, t // tk
    pairs = [(i, j) for i in range(nq) for j in range(((i + 1) * tq - 1) // tk + 1)]
    qi = jnp.asarray([p[0] for p in pairs], I32)
    kj = jnp.asarray([p[1] for p in pairs], I32)
    rows = MLA_HEADS * tq
    grid_spec = pltpu.PrefetchScalarGridSpec(
        num_scalar_prefetch=2,
        grid=(bsz, len(pairs)),
        in_specs=[pl.BlockSpec((tq, MLA_HEADS * KCAT), lambda b, p, qi, kj: (b * nq + qi[p], 0)),
                  pl.BlockSpec((tk, KCAT), lambda b, p, qi, kj: (b * nk + kj[p], 0))],
        out_specs=pl.BlockSpec((tq, MLA_HEADS * KV_LORA), lambda b, p, qi, kj: (b * nq + qi[p], 0)),
        scratch_shapes=[pltpu.VMEM((rows, KCAT), BF16), pltpu.VMEM((rows, LANES), F32),
                        pltpu.VMEM((rows, LANES), F32), pltpu.VMEM((rows, KV_LORA), F32)],
    )
    return pl.pallas_call(
        functools.partial(_attn_prompt_kernel, tq=tq, tk=tk),
        grid_spec=grid_spec,
        out_shape=jax.ShapeDtypeStruct((bsz * t, MLA_HEADS * KV_LORA), BF16),
        compiler_params=_cparams("parallel", "arbitrary"),
        name="attn_prompt",
    )(qi, kj, qcat, kcat)


def _attn_sample_kernel(pt_ref, q_ref, kn_ref, lat_hbm, kr_hbm, o_ref, lat_buf, kr_buf, sems, m_scr, l_scr, acc_scr,
                        *, layer, n_pages, t):
    b = pl.program_id(0)
    nb = pl.num_programs(0)
    slot = b % 2
    rows = MLA_HEADS * t

    def page_copies(bb, sl, i):
        pg = pt_ref[bb, i]
        return (pltpu.make_async_copy(lat_hbm.at[layer, pg], lat_buf.at[sl, i * PAGE_SIZE:(i + 1) * PAGE_SIZE, :],
                                      sems.at[0, sl]),
                pltpu.make_async_copy(kr_hbm.at[layer, pg], kr_buf.at[sl, :, i * PAGE_SIZE:(i + 1) * PAGE_SIZE],
                                      sems.at[1, sl]))

    def start_fetch(bb, sl):
        for i in range(n_pages):
            for cp in page_copies(bb, sl, i):
                cp.start()

    @pl.when(b == 0)
    def _():
        start_fetch(0, 0)

    @pl.when(b + 1 < nb)
    def _():
        start_fetch(b + 1, 1 - slot)

    m_scr[...] = jnp.full((rows, LANES), -jnp.inf, F32)
    l_scr[...] = jnp.zeros((rows, LANES), F32)
    acc_scr[...] = jnp.zeros((rows, KV_LORA), F32)
    qs = jnp.concatenate([q_ref[:, h * KCAT:(h + 1) * KCAT] for h in range(MLA_HEADS)], axis=0).astype(BF16)

    for i in range(n_pages):
        for cp in page_copies(b, slot, i):
            cp.wait()

    klat = lat_buf[slot].astype(BF16)
    krope_t = kr_buf[slot].astype(BF16)
    s = _dot_nt(qs[:, :KV_LORA], klat) + _dot(qs[:, KV_LORA:KV_LORA + QK_ROPE], krope_t)
    _softmax_step(s, klat, m_scr, l_scr, acc_scr)

    kn = jnp.concatenate([kn_ref[...], jnp.zeros((LANES - t, KCAT), F32)], axis=0).astype(BF16)
    sn = _dot_nt(qs, kn)
    qpos = lax.broadcasted_iota(I32, (rows, LANES), 0) & (t - 1)
    kpos = lax.broadcasted_iota(I32, (rows, LANES), 1)
    _softmax_step(jnp.where(kpos <= qpos, sn, -jnp.inf), kn[:, :KV_LORA], m_scr, l_scr, acc_scr)
    inv = _lane_tile(1.0 / l_scr[...], KV_LORA // LANES)
    for h in range(MLA_HEADS):
        o_ref[:, h * KV_LORA:(h + 1) * KV_LORA] = acc_scr[h * t:(h + 1) * t, :] * inv[h * t:(h + 1) * t, :]


def _attn_sample(qcat, kcat, cache_lat, cache_kr, page_table, layer, bsz, t):
    n_pages = page_table.shape[1]
    past = n_pages * PAGE_SIZE
    rows = MLA_HEADS * t
    grid_spec = pltpu.PrefetchScalarGridSpec(
        num_scalar_prefetch=1,
        grid=(bsz,),
        in_specs=[pl.BlockSpec((None, t, MLA_HEADS * KCAT), lambda b, pt: (b, 0, 0)),
                  pl.BlockSpec((None, t, KCAT), lambda b, pt: (b, 0, 0)),
                  pl.BlockSpec(memory_space=pl.ANY), pl.BlockSpec(memory_space=pl.ANY)],
        out_specs=pl.BlockSpec((None, t, MLA_HEADS * KV_LORA), lambda b, pt: (b, 0, 0)),
        scratch_shapes=[pltpu.VMEM((2, past, KV_LORA), F32), pltpu.VMEM((2, QK_ROPE, past), F32),
                        pltpu.SemaphoreType.DMA((2, 2)),
                        pltpu.VMEM((rows, LANES), F32), pltpu.VMEM((rows, LANES), F32),
                        pltpu.VMEM((rows, KV_LORA), F32)],
    )
    out = pl.pallas_call(
        functools.partial(_attn_sample_kernel, layer=layer, n_pages=n_pages, t=t),
        grid_spec=grid_spec,
        out_shape=jax.ShapeDtypeStruct((bsz, t, MLA_HEADS * KV_LORA), F32),
        compiler_params=_cparams("arbitrary"),
        name="attn_sample",
    )(page_table, qcat.reshape(bsz, t, -1), kcat.reshape(bsz, t, -1), cache_lat, jnp.swapaxes(cache_kr, 2, 3))
    return out.reshape(bsz * t, -1)


def _mla_out_kernel(o_ref, wuv_ref, wo_ref, xr_ref, g_ref, b_ref, out_ref):
    o = o_ref[...].astype(BF16)
    parts = [_dot(o[:, h * KV_LORA:(h + 1) * KV_LORA], wuv_ref[h]).astype(BF16) for h in range(MLA_HEADS)]
    h = _dot(jnp.concatenate(parts, axis=1), wo_ref[...])
    out_ref[...] = _layer_norm(DEEPNORM_ALPHA * xr_ref[...] + h, g_ref[...], b_ref[...])


def _mla_out(o, wuv, wo, xres, g, b, tm):
    m = o.shape[0]
    full = lambda a: pl.BlockSpec(a.shape, lambda i: (0,) * a.ndim)
    row = lambda n: pl.BlockSpec((tm, n), lambda i: (i, 0))
    return pl.pallas_call(
        _mla_out_kernel,
        grid=(m // tm,),
        in_specs=[row(MLA_HEADS * KV_LORA), full(wuv), full(wo), row(D_MODEL), full(g), full(b)],
        out_specs=row(D_MODEL),
        out_shape=jax.ShapeDtypeStruct((m, D_MODEL), F32),
        compiler_params=_cparams("parallel"),
        name="mla_out",
    )(o, wuv, wo, xres, g, b)


def _route_kernel(x_ref, rh_ref, rl_ref, bias_ref, pa_ref, pb_ref, wa_ref, wb_ref, ng_ref, *, subs):
    tm = subs * MOE_SUB
    xh, xm = _split2(x_ref[...])
    rh = rh_ref[...]
    logits = _dot_nt(rh, xh) + _dot_nt(rh, xm) + _dot_nt(rl_ref[...], xh)
    aff = jax.nn.sigmoid(logits)
    sel = aff + bias_ref[...]
    sub = lax.broadcasted_iota(I32, (EXPERTS_PER_GROUP, tm), 0)
    for g in range(MOE_GROUPS):
        sl = slice(g * EXPERTS_PER_GROUP, (g + 1) * EXPERTS_PER_GROUP)
        sg, ag = sel[sl, :], aff[sl, :]
        m1 = jnp.max(sg, axis=0, keepdims=True)
        i1 = jnp.min(jnp.where(sg == m1, sub, EXPERTS_PER_GROUP), axis=0, keepdims=True)
        sg2 = jnp.where(sub == i1, -jnp.inf, sg)
        m2 = jnp.max(sg2, axis=0, keepdims=True)
        i2 = jnp.min(jnp.where(sg2 == m2, sub, EXPERTS_PER_GROUP), axis=0, keepdims=True)
        a1 = jnp.sum(jnp.where(sub == i1, ag, 0.0), axis=0, keepdims=True)
        a2 = jnp.sum(jnp.where(sub == i2, ag, 0.0), axis=0, keepdims=True)
        score = m1 + m2
        if g == 0:
            best, ea, eb, va, vb = score, i1, i2, a1, a2
        else:
            better = score > best
            best = jnp.where(better, score, best)
            ea = jnp.where(better, i1 + g * EXPERTS_PER_GROUP, ea)
            eb = jnp.where(better, i2 + g * EXPERTS_PER_GROUP, eb)
            va = jnp.where(better, a1, va)
            vb = jnp.where(better, a2, vb)
    den = va + vb
    wa_ref[...] = va / den
    wb_ref[...] = vb / den

    eid = lax.broadcasted_iota(I32, (N_EXPERTS, tm), 0)
    oh_a = jnp.where(eid == ea, 1.0, 0.0)
    oh_b = jnp.where(eid == eb, 1.0, 0.0)
    before = lax.broadcasted_iota(I32, (MOE_SUB, MOE_SUB), 0) < lax.broadcasted_iota(I32, (MOE_SUB, MOE_SUB), 1)
    upper = jnp.where(before, 1.0, 0.0).astype(BF16)
    lower = jnp.where(lax.broadcasted_iota(I32, (N_EXPERTS, N_EXPERTS), 1)
                      < lax.broadcasted_iota(I32, (N_EXPERTS, N_EXPERTS), 0), 1.0, 0.0).astype(BF16)
    lane = lax.broadcasted_iota(I32, (N_EXPERTS, LANES), 1)
    ng_all = jnp.zeros((N_EXPERTS, LANES), F32)
    pos_a, pos_b = [], []
    for s in range(subs):
        a = oh_a[:, s * MOE_SUB:(s + 1) * MOE_SUB]
        b = oh_b[:, s * MOE_SUB:(s + 1) * MOE_SUB]
        cnt_a = jnp.sum(a, axis=1, keepdims=True)
        cnt = cnt_a + jnp.sum(b, axis=1, keepdims=True)
        ng = jnp.floor((cnt + (MOE_GRAN - 1)) * (1.0 / MOE_GRAN))
        rows = jnp.broadcast_to(ng * MOE_GRAN, (N_EXPERTS, LANES)).astype(BF16)
        start = _dot(lower, rows)[:, :1]
        pos_a.append(jnp.sum(a * (start + _dot(a.astype(BF16), upper)), axis=0, keepdims=True))
        pos_b.append(jnp.sum(b * (start + cnt_a + _dot(b.astype(BF16), upper)), axis=0, keepdims=True))
        ng_all = jnp.where(lane == s, ng, ng_all)
    pa_ref[...] = jnp.concatenate(pos_a, axis=1).astype(I32)
    pb_ref[...] = jnp.concatenate(pos_b, axis=1).astype(I32)
    ng_ref[...] = ng_all


def _route(x, rh, rl, bias, subs):
    m = x.shape[0]
    tm = subs * MOE_SUB
    full = lambda a: pl.BlockSpec(a.shape, lambda i: (0, 0))
    lane_row = pl.BlockSpec((1, tm), lambda i: (0, i))
    row_i = jax.ShapeDtypeStruct((1, m), I32)
    row_f = jax.ShapeDtypeStruct((1, m), F32)
    return pl.pallas_call(
        functools.partial(_route_kernel, subs=subs),
        grid=(m // tm,),
        in_specs=[pl.BlockSpec((tm, D_MODEL), lambda i: (i, 0)), full(rh), full(rl), full(bias)],
        out_specs=[lane_row, lane_row, lane_row, lane_row,
                   pl.BlockSpec((None, N_EXPERTS, LANES), lambda i: (i, 0, 0))],
        out_shape=[row_i, row_i, row_f, row_f, jax.ShapeDtypeStruct((m // tm, N_EXPERTS, LANES), F32)],
        compiler_params=_cparams("parallel"),
        name="moe_route",
    )(x, rh, rl, bias)


def _moe_kernel(ng_ref, gs_ref, nit_ref, x_ref, pa_ref, pb_ref, wa_ref, wb_ref, wg_ref, wu_ref, wd_ref,
                g_ref, b_ref, o_ref, buf, *, ns, subs, eps):
    c = pl.program_id(0)
    st = pl.program_id(1)
    n_sort = ns // subs
    n_exp = N_EXPERTS // eps
    row_id = lax.broadcasted_iota(I32, (MOE_ROWS, MOE_SUB), 0)

    @pl.when(st < n_sort)
    def _():
        for k in range(subs):
            sl = slice(k * MOE_SUB, (k + 1) * MOE_SUB)
            hit = jnp.logical_or(row_id == pa_ref[:, sl], row_id == pb_ref[:, sl])
            perm = jnp.where(hit, 1.0, 0.0).astype(BF16)
            s = st * subs + k
            buf[s, 0:MOE_ROWS, :] = _dot(perm, x_ref[sl, :].astype(BF16)).astype(BF16)
            buf[s, MOE_ROWS:MOE_ROWS + MOE_GRAN, :] = jnp.zeros((MOE_GRAN, D_MODEL), BF16)

    @pl.when(jnp.logical_and(st >= n_sort, st < n_sort + n_exp))
    def _():
        e0 = (st - n_sort) * eps

        def body(w, carry):
            starts, xws = [], []
            for j in range(eps):
                row0s = []
                for s in range(ns):
                    idx = (c * ns + s) * N_EXPERTS + e0 + j
                    row0 = jnp.where(w < ng_ref[idx], gs_ref[idx] + w * MOE_GRAN, MOE_ROWS)
                    row0s.append(pl.multiple_of(row0, MOE_GRAN))
                starts.append(row0s)
                xws.append(jnp.concatenate([buf[s, pl.ds(row0s[s], MOE_GRAN), :] for s in range(ns)], axis=0))
            ys = []
            for j in range(eps):
                h = _silu(_dot(xws[j], wg_ref[j])) * _dot(xws[j], wu_ref[j])
                ys.append(_dot(h.astype(BF16), wd_ref[j]).astype(BF16))
            for j in range(eps):
                for s in range(ns):
                    buf[s, pl.ds(starts[j][s], MOE_GRAN), :] = ys[j][s * MOE_GRAN:(s + 1) * MOE_GRAN, :]
            return carry

        trips = nit_ref[c * N_EXPERTS + e0]
        for j in range(1, eps):
            trips = jnp.maximum(trips, nit_ref[c * N_EXPERTS + e0 + j])
        lax.fori_loop(0, trips, body, 0)

    @pl.when(st >= n_sort + n_exp)
    def _():
        for k in range(subs):
            sl = slice(k * MOE_SUB, (k + 1) * MOE_SUB)
            gate = (jnp.where(row_id == pa_ref[:, sl], wa_ref[:, sl], 0.0)
                    + jnp.where(row_id == pb_ref[:, sl], wb_ref[:, sl], 0.0)).astype(BF16)
            s = (st - n_sort - n_exp) * subs + k
            y = _dot_tn(gate, buf[s, 0:MOE_ROWS, :])
            o_ref[sl, :] = _layer_norm(DEEPNORM_ALPHA * x_ref[sl, :] + y, g_ref[...], b_ref[...])


def _hier_moe_ln(x, prm, g, b, ns):
    m = x.shape[0]
    subs, eps = MOE_SUBS_PER_STEP, MOE_EXPERTS_PER_STEP
    tm = subs * MOE_SUB
    chunk = ns * MOE_SUB
    n_chunks = m // chunk
    n_sort = ns // subs
    n_exp = N_EXPERTS // eps
    pa, pb, wa, wb, ng = _route(x, prm["router_hi"], prm["router_lo"], prm["router_bias"], subs)
    ng = jnp.transpose(ng[:, :, :subs].astype(I32), (0, 2, 1)).reshape(n_chunks, ns, N_EXPERTS)
    gstart = (jnp.cumsum(ng, axis=-1) - ng) * MOE_GRAN
    nit = jnp.max(ng, axis=1)

    def tok_idx(c, st, *_):
        phase = jnp.where(st < n_sort, st, jnp.where(st >= n_sort + n_exp, st - n_sort - n_exp, n_sort - 1))
        return c * n_sort + phase

    def exp_idx(c, st, *_):
        return (prm["layer"], jnp.clip(st - n_sort, 0, n_exp - 1), 0, 0)

    lane_row = pl.BlockSpec((1, tm), lambda c, st, *_: (0, tok_idx(c, st)))
    full = lambda a: pl.BlockSpec(a.shape, lambda c, st, *_: (0, 0))
    grid_spec = pltpu.PrefetchScalarGridSpec(
        num_scalar_prefetch=3,
        grid=(n_chunks, 2 * n_sort + n_exp),
        in_specs=[pl.BlockSpec((tm, D_MODEL), lambda c, st, *_: (tok_idx(c, st), 0)),
                  lane_row, lane_row, lane_row, lane_row,
                  pl.BlockSpec((None, eps, D_MODEL, EXPERT_FF), exp_idx),
                  pl.BlockSpec((None, eps, D_MODEL, EXPERT_FF), exp_idx),
                  pl.BlockSpec((None, eps, EXPERT_FF, D_MODEL), exp_idx),
                  full(g), full(b)],
        out_specs=pl.BlockSpec((tm, D_MODEL),
                               lambda c, st, *_: (c * n_sort + jnp.clip(st - n_sort - n_exp, 0, n_sort - 1), 0)),
        scratch_shapes=[pltpu.VMEM((ns, MOE_ROWS + MOE_GRAN, D_MODEL), BF16)],
    )
    return pl.pallas_call(
        functools.partial(_moe_kernel, ns=ns, subs=subs, eps=eps),
        grid_spec=grid_spec,
        out_shape=jax.ShapeDtypeStruct((m, D_MODEL), F32),
        compiler_params=_cparams("arbitrary", "arbitrary"),
        name="moe_experts",
    )(ng.reshape(-1), gstart.reshape(-1), nit.reshape(-1), x, pa, pb, wa, wb,
      prm["w_gate"], prm["w_up"], prm["w_down"], g, b)


def _pad_cols(a, n):
    return jnp.pad(a, ((0, 0), (0, n - a.shape[1])))


def _ssd_params(w_in, conv_w, conv_b, dt_bias, a_log, d_skip, norm_w, w_out):
    wdt = _pad_cols(w_in[:, D_INNER + CONV_DIM:], LANES)
    wdh = wdt.astype(BF16)
    head_of_col = jnp.arange(D_INNER, dtype=I32) // SSD_HEAD_DIM
    expand = (jnp.arange(LANES, dtype=I32)[:, None] == head_of_col[None, :]).astype(BF16)
    return {
        "wz": w_in[:, :D_INNER].astype(BF16),
        "wx": w_in[:, D_INNER:D_INNER + CONV_DIM].astype(BF16),
        "wdh": wdh,
        "wdl": (wdt - wdh.astype(F32)).astype(BF16),
        "conv_w": conv_w,
        "conv_b": conv_b[None, :],
        "dt_bias": _pad_cols(dt_bias[None, :], LANES),
        "a_log": _pad_cols(a_log[None, :], LANES),
        "d_skip": jnp.repeat(d_skip, SSD_HEAD_DIM)[None, :],
        "norm_w": norm_w[None, :],
        "expand": expand,
        "w_out": w_out.astype(BF16),
    }


def _mla_params(w_dqkv, q_norm, kv_norm, w_uq, w_uk, w_uv, w_o):
    wr = jnp.pad(w_uq[:, :, QK_NOPE:], ((0, 0), (0, 0), (0, LANES - QK_ROPE)))
    return {
        "w_down": _pad_cols(w_dqkv, DCOLS).astype(BF16),
        "q_norm": q_norm[None, :],
        "kv_norm": kv_norm[None, :],
        "w_qn": w_uq[:, :, :QK_NOPE].reshape(Q_LORA, MLA_HEADS * QK_NOPE).astype(BF16),
        "w_qr": wr.reshape(Q_LORA, MLA_HEADS * LANES).astype(BF16),
        "w_uk_t": jnp.transpose(w_uk, (1, 2, 0)).astype(BF16),
        "w_uv": jnp.transpose(w_uv, (1, 0, 2)).astype(BF16),
        "w_o": w_o.astype(BF16),
    }


def _moe_params(layer, router, router_bias, w_gate, w_up, w_down):
    rt = router.T
    rh = rt.astype(BF16)
    return {
        "layer": layer,
        "router_hi": rh,
        "router_lo": (rt - rh.astype(F32)).astype(BF16),
        "router_bias": router_bias[:, None],
        "w_gate": w_gate,
        "w_up": w_up,
        "w_down": w_down,
    }


def _rope_tables(pos):
    half = QK_ROPE // 2
    inv_freq = ROPE_THETA ** (-jnp.arange(half, dtype=F32) / half)
    ang = pos.astype(F32)[:, None] * inv_freq[None, :]
    cos, sin = jnp.cos(ang), jnp.sin(ang)
    zeros = jnp.zeros((pos.shape[0], LANES - QK_ROPE), F32)
    return jnp.concatenate([cos, cos, zeros], axis=1), jnp.concatenate([-sin, sin, zeros], axis=1)


def _tile(m, want):
    return min(m, want)


def kernel(x_prompt, x_sample, cache_mla_latent, cache_mla_krope, state_ssd, state_conv, page_table,
           ln_mix_g, ln_mix_b, ln_ffn_g, ln_ffn_b,
           ssd_w_in, ssd_conv_w, ssd_conv_b, ssd_dt_bias, ssd_a_log, ssd_d_skip, ssd_norm_w, ssd_w_out,
           mla_w_dqkv, mla_q_norm, mla_kv_norm, mla_w_uq, mla_w_uk, mla_w_uv, mla_w_o,
           moe_router, moe_router_bias, moe_w_gate, moe_w_up, moe_w_down):
    bp, tp, _ = x_prompt.shape
    bs, ts, _ = x_sample.shape
    past_len = page_table.shape[1] * PAGE_SIZE
    xp = x_prompt.reshape(bp * tp, D_MODEL)
    xs = x_sample.reshape(bs * ts, D_MODEL)
    lp = min(SSD_CHUNK, tp)

    cos_p, sin_p = _rope_tables(jnp.arange(tp))
    cos_s, sin_s = _rope_tables(past_len + jnp.arange(ts))
    cos_s = jnp.tile(cos_s, (bs, 1))
    sin_s = jnp.tile(sin_s, (bs, 1))

    w_gate_b, w_up_b, w_down_b = moe_w_gate.astype(BF16), moe_w_up.astype(BF16), moe_w_down.astype(BF16)
    p_lat, p_kr, p_ssm, p_conv = [], [], [], []
    s_lat, s_kr, s_conv = [], [], []
    s_ssm = jnp.zeros((state_ssd.shape[0], bs) + state_ssd.shape[2:], F32)
    for i in range(DEPTH):
        j = i // 2
        g_mix, b_mix = ln_mix_g[i][None, :], ln_mix_b[i][None, :]
        if i % 2 == 0:
            prm = _ssd_params(ssd_w_in[j], ssd_conv_w[j], ssd_conv_b[j], ssd_dt_bias[j], ssd_a_log[j],
                              ssd_d_skip[j], ssd_norm_w[j], ssd_w_out[j])
            tm = _tile(tp, 256)
            zs, dt, xact, bc, tails = _ssd_inproj(xp, prm, BF16, tm, seq_len=tp)
            y, sp = _ssd_core([xact, bc], zs, dt, prm, bp, tp, lp, BF16)
            cp = tails.reshape(bp, tp // tm, 8, CONV_DIM)[:, -1, 8 - (SSD_CONV - 1):, :]
            xp = _proj_ln(y, prm["w_out"], xp, g_mix, b_mix, _tile(xp.shape[0], 512))
            zs, dt, xbc = _ssd_inproj(xs, prm, F32, _tile(xs.shape[0], 256))
            y, cs, s_ssm = _ssd_core([xbc], zs, dt, prm, bs, ts, ts, F32, init=(state_conv, state_ssd, j),
                                     stacked=s_ssm)
            xs = _proj_ln(y, prm["w_out"], xs, g_mix, b_mix, _tile(xs.shape[0], 512))
            p_conv.append(cp)
            p_ssm.append(sp)
            s_conv.append(cs)
        else:
            prm = _mla_params(mla_w_dqkv[j], mla_q_norm[j], mla_kv_norm[j], mla_w_uq[j], mla_w_uk[j],
                              mla_w_uv[j], mla_w_o[j])
            tm = _tile(tp, 512)
            cq, lat, kr, kcat = _mla_down(xp, prm["w_down"], prm["q_norm"], prm["kv_norm"], cos_p, sin_p, BF16, tm)
            qcat = _mla_q(cq, prm["w_qn"], prm["w_qr"], prm["w_uk_t"], cos_p, sin_p, BF16, tm)
            o = _attn_prompt(qcat, kcat, bp, tp, tm, tm)
            xp = _mla_out(o, prm["w_uv"], prm["w_o"], xp, g_mix, b_mix, tm)
            p_lat.append(lat.reshape(bp, tp, KV_LORA))
            p_kr.append(kr.reshape(bp, tp, QK_ROPE))
            tm = _tile(xs.shape[0], 512)
            cq, lat, kr, kcat = _mla_down(xs, prm["w_down"], prm["q_norm"], prm["kv_norm"], cos_s, sin_s, F32, tm)
            qcat = _mla_q(cq, prm["w_qn"], prm["w_qr"], prm["w_uk_t"], cos_s, sin_s, F32, tm)
            o = _attn_sample(qcat, kcat, cache_mla_latent, cache_mla_krope, page_table, j, bs, ts)
            xs = _mla_out(o, prm["w_uv"], prm["w_o"], xs, g_mix, b_mix, tm)
            s_lat.append(lat.reshape(bs, ts, KV_LORA))
            s_kr.append(kr.reshape(bs, ts, QK_ROPE))
        mprm = _moe_params(i, moe_router[i], moe_router_bias[i], w_gate_b, w_up_b, w_down_b)
        g_ffn, b_ffn = ln_ffn_g[i][None, :], ln_ffn_b[i][None, :]
        xp = _hier_moe_ln(xp, mprm, g_ffn, b_ffn, min(8, xp.shape[0] // MOE_SUB))
        xs = _hier_moe_ln(xs, mprm, g_ffn, b_ffn, min(8, xs.shape[0] // MOE_SUB))

    return (xp.reshape(bp, tp, D_MODEL), xs.reshape(bs, ts, D_MODEL),
            jnp.stack(p_lat), jnp.stack(p_kr), jnp.stack(p_ssm), jnp.stack(p_conv),
            jnp.stack(s_lat), jnp.stack(s_kr), s_ssm, jnp.stack(s_conv))
```
